```python
import math
import jax, jax.numpy as jnp
from jax import lax
import numpy as np

D_MODEL = 1024
BATCH = 16
SEQ = 4096
DEPTH = 1
DEC_BATCH = 128
DEC_SEQ = 8
PAST_LEN = 8192
PAGE_SIZE = 128

D_POOL = D_MODEL // 2
POOL_WINDOWS = (2, 4, 8, 16)
N_POOL_GROUPS = len(POOL_WINDOWS)
POOL_GROUP_W = D_POOL // N_POOL_GROUPS
POOL_STATE_LEN = max(POOL_WINDOWS) - 1
D_ATTN = D_MODEL // 2
N_HEADS = 4
HEAD_DIM_V = D_ATTN // N_HEADS
HEAD_DIM_QK = HEAD_DIM_V // 2
ATTN_SCALE = HEAD_DIM_QK ** -0.5
Q_BLOCK = 128
D_MIX = D_POOL + D_ATTN
D_IN_PROJ = 2 * D_POOL + 4 * D_ATTN
SPLITS = [D_POOL, 2 * D_POOL, 2 * D_POOL + D_ATTN, 2 * D_POOL + 2 * D_ATTN, 2 * D_POOL + 3 * D_ATTN]
D_PLE = 256
EPS = 1e-6

kernel_name = "hybrid_pool_diffattn_decoder_step"


def lambda_init(layer):
    return 0.8 - 0.6 * math.exp(-0.3 * layer)


def rmsnorm(x, g):
    xf = x.astype(jnp.float32)
    r = xf * lax.rsqrt(jnp.mean(xf * xf, axis=-1, keepdims=True) + EPS)
    return (r * g.astype(jnp.float32)).astype(x.dtype)


def alibi_slopes():
    return jnp.exp2(-8.0 * jnp.arange(1, N_HEADS + 1, dtype=jnp.float32) / N_HEADS)


def project_in(h, norm_g, w_in):
    a = rmsnorm(h, norm_g)
    proj = a @ w_in
    u, z_pool, q, k, v, z_attn = jnp.split(proj, SPLITS, axis=-1)
    lead = h.shape[:-1]
    q = q.reshape(*lead, N_HEADS, 2, HEAD_DIM_QK)
    k = k.reshape(*lead, N_HEADS, 2, HEAD_DIM_QK)
    v = v.reshape(*lead, N_HEADS, HEAD_DIM_V)
    return u, z_pool, q, k, v, z_attn


def pool_mix(u_ext, pos0, w_pool, s_pool):
    B, L, _ = u_ext.shape
    uf = u_ext.astype(jnp.float32)
    cs = jnp.cumsum(uf, axis=1)
    pos = pos0 + jnp.arange(L)
    means = []
    for g, w in enumerate(POOL_WINDOWS):
        csg = cs[..., g * POOL_GROUP_W:(g + 1) * POOL_GROUP_W]
        lagged = jnp.pad(csg, ((0, 0), (w, 0), (0, 0)))[:, :L]
        cnt = jnp.minimum(pos + 1, w).astype(jnp.float32)[:, None]
        means.append((csg - lagged) / cnt)
    pooled = (jnp.concatenate(means, axis=-1) - uf).astype(u_ext.dtype)
    pooled = pooled.reshape(B, L, N_POOL_GROUPS, POOL_GROUP_W)
    y = jnp.einsum('blgc,gcd->blgd', pooled, w_pool).reshape(B, L, D_POOL)
    return y * s_pool


def diff_attend(q, k, v, qpos, kpos, lam, slopes):
    s = jnp.einsum('...qhmd,...khmd->...hmqk', q, k,
                   preferred_element_type=jnp.float32) * ATTN_SCALE
    dist = qpos[:, None] - kpos[None, :]
    bias = -slopes[:, None, None, None] * dist.astype(jnp.float32)
    s = jnp.where(dist >= 0, s + bias, -jnp.inf)
    p = jax.nn.softmax(s, axis=-1)
    a = p[..., 0, :, :] - lam * p[..., 1, :, :]
    return jnp.einsum('...hqk,...khd->...qhd', a.astype(v.dtype), v)


def attend_prompt(q, k, v, lam, slopes):
    B, S = q.shape[:2]
    nb = S // Q_BLOCK
    qb = q.reshape(B, nb, Q_BLOCK, N_HEADS, 2, HEAD_DIM_QK).swapaxes(0, 1)
    kpos = jnp.arange(S)

    def block(args):
        qi, i = args
        qpos = i * Q_BLOCK + jnp.arange(Q_BLOCK)
        return diff_attend(qi, k, v, qpos, kpos, lam, slopes)

    o = lax.map(block, (qb, jnp.arange(nb)))
    return o.swapaxes(0, 1).reshape(B, S, N_HEADS, HEAD_DIM_V)


def attend_sample(q, k_new, v_new, cache_k, cache_v, page_table, lam, slopes):
    past = page_table.shape[1] * PAGE_SIZE
    T = q.shape[1]
    qpos = past + jnp.arange(T)
    kpos = jnp.arange(past + T)

    def one(args):
        qb, kb, vb, pt = args
        kp = cache_k[pt].reshape(past, N_HEADS, 2, HEAD_DIM_QK)
        vp = cache_v[pt].reshape(past, N_HEADS, HEAD_DIM_V)
        return diff_attend(qb, jnp.concatenate([kp, kb], axis=0),
                           jnp.concatenate([vp, vb], axis=0), qpos, kpos, lam, slopes)

    return lax.map(one, (q, k_new, v_new, page_table))


def merge_out(h, y_pool, z_pool, o_attn, z_attn, w_out, p, w_ple_gate, w_ple_proj):
    ya = y_pool * jax.nn.silu(z_pool)
    yb = o_attn.reshape(*o_attn.shape[:-2], D_ATTN) * jax.nn.silu(z_attn)
    h = h + jnp.concatenate([ya, yb], axis=-1) @ w_out
    return h + jax.nn.sigmoid(h @ w_ple_gate) * (p @ w_ple_proj)


def setup_inputs(seed: int = 0) -> dict:
    key = jax.random.key(seed)
    ks = jax.random.split(key, 24)
    f32 = jnp.float32
    n_pages = PAST_LEN // PAGE_SIZE
    n_used = DEC_BATCH * n_pages
    n_phys = n_used + n_used // 4
    nrm = lambda k, shape, s=1.0: jax.random.normal(k, shape, f32) * s
    page_table = jax.random.permutation(ks[0], n_phys)[:n_used].reshape(DEC_BATCH, n_pages).astype(jnp.int32)
    return {
        "x_prompt": nrm(ks[1], (BATCH, SEQ, D_MODEL)),
        "x_sample": nrm(ks[2], (DEC_BATCH, DEC_SEQ, D_MODEL)),
        "cache_k": nrm(ks[3], (DEPTH, n_phys, PAGE_SIZE, N_HEADS, 2, HEAD_DIM_QK)),
        "cache_v": nrm(ks[4], (DEPTH, n_phys, PAGE_SIZE, N_HEADS, HEAD_DIM_V)),
        "state_pool": nrm(ks[5], (DEPTH, DEC_BATCH, POOL_STATE_LEN, D_POOL)),
        "page_table": page_table,
        "p_prompt": nrm(ks[6], (DEPTH, BATCH, SEQ, D_PLE)),
        "p_sample": nrm(ks[7], (DEPTH, DEC_BATCH, DEC_SEQ, D_PLE)),
        "norm_in": 1.0 + nrm(ks[8], (DEPTH, D_MODEL), 0.05),
        "w_in": nrm(ks[9], (DEPTH, D_MODEL, D_IN_PROJ), D_MODEL ** -0.5),
        "w_pool": nrm(ks[10], (DEPTH, N_POOL_GROUPS, POOL_GROUP_W, POOL_GROUP_W), POOL_GROUP_W ** -0.5),
        "s_pool": 1.0 + nrm(ks[11], (DEPTH, D_POOL), 0.1),
        "lambda_q1": nrm(ks[12], (DEPTH, HEAD_DIM_QK), 0.1),
        "lambda_k1": nrm(ks[13], (DEPTH, HEAD_DIM_QK), 0.1),
        "lambda_q2": nrm(ks[14], (DEPTH, HEAD_DIM_QK), 0.1),
        "lambda_k2": nrm(ks[15], (DEPTH, HEAD_DIM_QK), 0.1),
        "head_norm": 1.0 + nrm(ks[16], (DEPTH, HEAD_DIM_V), 0.05),
        "w_out": nrm(ks[17], (DEPTH, D_MIX, D_MODEL), D_MIX ** -0.5),
        "w_ple_gate": nrm(ks[18], (DEPTH, D_MODEL, D_MODEL), D_MODEL ** -0.5),
        "w_ple_proj": nrm(ks[19], (DEPTH, D_PLE, D_MODEL), D_PLE ** -0.5),
        "norm_final": 1.0 + nrm(ks[20], (D_MODEL,), 0.05),
    }


def reference(x_prompt, x_sample, cache_k, cache_v, state_pool, page_table, p_prompt, p_sample,
              norm_in, w_in, w_pool, s_pool, lambda_q1, lambda_k1, lambda_q2, lambda_k2,
              head_norm, w_out, w_ple_gate, w_ple_proj, norm_final):
    slopes = alibi_slopes()
    past = page_table.shape[1] * PAGE_SIZE
    hp, hs = x_prompt, x_sample
    k_p, v_p, pool_p, k_s, v_s, pool_s = [], [], [], [], [], []
    for i in range(DEPTH):
        lam_init = lambda_init(i)
        lam = (jnp.exp(jnp.sum(lambda_q1[i].astype(jnp.float32) * lambda_k1[i].astype(jnp.float32)))
               - jnp.exp(jnp.sum(lambda_q2[i].astype(jnp.float32) * lambda_k2[i].astype(jnp.float32)))
               + lam_init)
        u, zp, q, k, v, za = project_in(hp, norm_in[i], w_in[i])
        y_pool = pool_mix(u, 0, w_pool[i], s_pool[i])
        o = attend_prompt(q, k, v, lam, slopes)
        o = rmsnorm(o, head_norm[i]) * (1.0 - lam_init)
        hp = merge_out(hp, y_pool, zp, o, za, w_out[i], p_prompt[i], w_ple_gate[i], w_ple_proj[i])
        k_p.append(k)
        v_p.append(v)
        pool_p.append(u[:, -POOL_STATE_LEN:])
        u, zp, q, k, v, za = project_in(hs, norm_in[i], w_in[i])
        u_ext = jnp.concatenate([state_pool[i], u], axis=1)
        y_pool = pool_mix(u_ext, past - POOL_STATE_LEN, w_pool[i], s_pool[i])[:, POOL_STATE_LEN:]
        o = attend_sample(q, k, v, cache_k[i], cache_v[i], page_table, lam, slopes)
        o = rmsnorm(o, head_norm[i]) * (1.0 - lam_init)
        hs = merge_out(hs, y_pool, zp, o, za, w_out[i], p_sample[i], w_ple_gate[i], w_ple_proj[i])
        k_s.append(k)
        v_s.append(v)
        pool_s.append(u_ext[:, -POOL_STATE_LEN:])
    y_prompt = rmsnorm(hp, norm_final)
    y_sample = rmsnorm(hs, norm_final)
    return (y_prompt, y_sample, jnp.stack(k_p), jnp.stack(v_p), jnp.stack(pool_p),
            jnp.stack(k_s), jnp.stack(v_s), jnp.stack(pool_s))
```

```python
import functools
import math

import jax
import jax.numpy as jnp
from jax import lax
from jax.experimental import pallas as pl
from jax.experimental.pallas import tpu as pltpu

F32 = jnp.float32
BF16 = jnp.bfloat16

EPS = 1e-6
POOL_WINDOWS = (2, 4, 8, 16)
POOL_STATE_LEN = max(POOL_WINDOWS) - 1
HALO = 16
N_HEADS = 4
LANES = 128
VMEM_LIMIT = 56 * 1024 * 1024


def _lambda_init(layer):
    return 0.8 - 0.6 * math.exp(-0.3 * layer)


def _alibi_slope(h):
    return 2.0 ** (-8.0 * (h + 1) / N_HEADS)


def _rmsnorm(xf, g):
    return xf * lax.rsqrt(jnp.mean(xf * xf, axis=-1, keepdims=True) + EPS) * g


def _lam_value(lq1, lk1, lq2, lk2, lam_init):
    a = jnp.sum(lq1[...] * lk1[...], axis=-1, keepdims=True)
    b = jnp.sum(lq2[...] * lk2[...], axis=-1, keepdims=True)
    return jnp.exp(a) - jnp.exp(b) + lam_init


def _window_pool(ext_ref, base, rows, inv_cnt):
    outs = []
    for g, w in enumerate(POOL_WINDOWS):
        cols = slice(g * LANES, (g + 1) * LANES)
        cur = ext_ref[base:base + rows, cols]
        acc = cur
        for j in range(1, w):
            acc = acc + ext_ref[base - j:base - j + rows, cols]
        outs.append(acc * inv_cnt[g] - cur)
    return outs


def _inproj_body(x_ref, g_ref, w_ref, *refs, tm, d_split, attn_scale, prompt_pool):
    if prompt_pool:
        (pooled_ref, tail_ref, zp_ref, q_ref, k_ref, v_ref, za_ref, kb_ref, vb_ref, ext_ref) = refs
    else:
        (u_ref, zp_ref, q_ref, k_ref, v_ref, za_ref) = refs
    a = _rmsnorm(x_ref[...], g_ref[...]).astype(BF16)

    def proj(c):
        return jnp.dot(a, w_ref[:, c * d_split:(c + 1) * d_split], preferred_element_type=F32)

    u = proj(0)
    zp_ref[...] = proj(1)
    q_ref[...] = (proj(2) * attn_scale).astype(BF16)
    k = proj(3)
    k_ref[...] = k
    v = proj(4)
    v_ref[...] = v
    za_ref[...] = proj(5)
    if not prompt_pool:
        u_ref[...] = u
        return
    kb_ref[...] = k.astype(BF16)
    vb_ref[...] = v.astype(BF16)
    i = pl.program_id(1)

    @pl.when(i == 0)
    def _():
        ext_ref[0:HALO, :] = jnp.zeros((HALO, d_split), F32)

    ext_ref[HALO:HALO + tm, :] = u
    pos = i * tm + lax.broadcasted_iota(jnp.int32, (tm, 1), 0)
    inv_cnt = [1.0 / jnp.minimum(pos + 1, w).astype(F32) for w in POOL_WINDOWS]
    pooled = _window_pool(ext_ref, HALO, tm, inv_cnt)
    for g in range(len(POOL_WINDOWS)):
        pooled_ref[0, :, g * LANES:(g + 1) * LANES] = pooled[g]
    tail = ext_ref[tm:tm + HALO, :]
    tail_ref[0] = tail
    ext_ref[0:HALO, :] = tail


def _inproj_prompt(x, g, w_in, *, tm):
    B, S, D = x.shape
    d_split = w_in.shape[1] // 6
    nt = S // tm
    tok = lambda b, i: (b, i, 0)
    tok_spec = pl.BlockSpec((1, tm, d_split), tok)
    flat_spec = pl.BlockSpec((tm, d_split), lambda b, i: (b * nt + i, 0))
    f32_tok = jax.ShapeDtypeStruct((B * S, d_split), F32)
    bf_tok = jax.ShapeDtypeStruct((B * S, d_split), BF16)
    body = functools.partial(_inproj_body, tm=tm, d_split=d_split, attn_scale=(d_split // N_HEADS // 2) ** -0.5,
                             prompt_pool=True)
    return pl.pallas_call(
        body,
        grid=(B, nt),
        in_specs=[pl.BlockSpec((tm, D), lambda b, i: (b * nt + i, 0)),
                  pl.BlockSpec((1, D), lambda b, i: (0, 0)),
                  pl.BlockSpec(w_in.shape, lambda b, i: (0, 0))],
        out_specs=[tok_spec,
                   pl.BlockSpec((1, HALO, d_split), lambda b, i: (b, 0, 0)),
                   flat_spec, flat_spec, flat_spec, flat_spec, flat_spec, flat_spec, flat_spec],
        out_shape=[jax.ShapeDtypeStruct((B, S, d_split), F32),
                   jax.ShapeDtypeStruct((B, HALO, d_split), F32),
                   f32_tok, bf_tok, f32_tok, f32_tok, f32_tok, bf_tok, bf_tok],
        scratch_shapes=[pltpu.VMEM((HALO + tm, d_split), F32)],
        compiler_params=pltpu.CompilerParams(dimension_semantics=("arbitrary", "arbitrary"),
                                             vmem_limit_bytes=VMEM_LIMIT),
        name="inproj_prompt",
    )(x.reshape(B * S, D), g, w_in)


def _inproj_sample(x2d, g, w_in, *, tm):
    T, D = x2d.shape
    d_split = w_in.shape[1] // 6
    spec = pl.BlockSpec((tm, d_split), lambda i: (i, 0))
    f32_tok = jax.ShapeDtypeStruct((T, d_split), F32)
    body = functools.partial(_inproj_body, tm=tm, d_split=d_split, attn_scale=(d_split // N_HEADS // 2) ** -0.5,
                             prompt_pool=False)
    return pl.pallas_call(
        body,
        grid=(T // tm,),
        in_specs=[pl.BlockSpec((tm, D), lambda i: (i, 0)),
                  pl.BlockSpec((1, D), lambda i: (0, 0)),
                  pl.BlockSpec(w_in.shape, lambda i: (0, 0))],
        out_specs=[spec] * 6,
        out_shape=[f32_tok, f32_tok, jax.ShapeDtypeStruct((T, d_split), BF16), f32_tok, f32_tok, f32_tok],
        compiler_params=pltpu.CompilerParams(dimension_semantics=("arbitrary",), vmem_limit_bytes=VMEM_LIMIT),
        name="inproj_sample",
    )(x2d, g, w_in)


def _softmax_step(s, vb, m_i, l_i, acc):
    m_new = jnp.maximum(m_i, jnp.max(s, axis=1, keepdims=True))
    alpha = jnp.exp(m_i - m_new)
    p = jnp.exp(s - m_new)
    l_new = alpha * l_i + jnp.sum(p, axis=1, keepdims=True)
    acc_new = alpha * acc + jnp.dot(p.astype(BF16), vb, preferred_element_type=F32)
    return m_new, l_new, acc_new


def _head_finish(o1, l1, o2, l2, lam, hn, out_scale):
    o = o1 / l1 - lam * (o2 / l2)
    return _rmsnorm(o, hn) * out_scale


def _attn_prompt_body(lq1, lk1, lq2, lk2, hn_ref, q_ref, k_ref, v_ref, o_ref, *, tq, lam_init):
    i = pl.program_id(1)
    lam = _lam_value(lq1, lk1, lq2, lk2, lam_init)
    hn = hn_ref[...]
    rows = 2 * tq
    half = LANES // 2
    lane = lax.broadcasted_iota(jnp.int32, (tq, LANES), 1)
    r_i = lax.broadcasted_iota(jnp.int32, (rows, tq), 0)
    c_i = lax.broadcasted_iota(jnp.int32, (rows, tq), 1)
    q_loc = jnp.where(r_i >= tq, r_i - tq, r_i)
    rel = (c_i - q_loc).astype(F32)
    visible = c_i <= q_loc
    for h in range(N_HEADS):
        slope = _alibi_slope(h)
        hs = slice(h * LANES, (h + 1) * LANES)
        qh = q_ref[0, :, hs]
        zero = jnp.zeros_like(qh)
        q2 = jnp.concatenate([jnp.where(lane < half, qh, zero), jnp.where(lane >= half, qh, zero)], axis=0)
        bias_loc = slope * rel

        def scores(j):
            kb = k_ref[0, pl.ds(pl.multiple_of(j * tq, tq), tq), hs]
            s = lax.dot_general(q2, kb, (((1,), (1,)), ((), ())), preferred_element_type=F32)
            return s + (bias_loc + slope * ((j - i) * tq).astype(F32))

        def values(j):
            return v_ref[0, pl.ds(pl.multiple_of(j * tq, tq), tq), hs]

        def full_block(j, carry):
            return _softmax_step(scores(j), values(j), *carry)

        init = (jnp.full((rows, 1), -jnp.inf, F32), jnp.zeros((rows, 1), F32), jnp.zeros((rows, LANES), F32))
        carry = lax.fori_loop(0, i, full_block, init)
        s_diag = jnp.where(visible, scores(i), -jnp.inf)
        m_f, l_f, acc = _softmax_step(s_diag, values(i), *carry)
        o = _head_finish(acc[:tq], l_f[:tq], acc[tq:], l_f[tq:], lam, hn, 1.0 - lam_init)
        o_ref[0, :, hs] = o


def _attn_prompt(lams, hn, qb, kb, vb, *, tq, lam_init):
    B, S, W = qb.shape
    body = functools.partial(_attn_prompt_body, tq=tq, lam_init=lam_init)
    small = lambda a: pl.BlockSpec(a.shape, lambda b, i: (0, 0))
    seq_spec = pl.BlockSpec((1, S, W), lambda b, i: (b, 0, 0))
    return pl.pallas_call(
        body,
        grid=(B, S // tq),
        in_specs=[small(l) for l in lams] + [small(hn), pl.BlockSpec((1, tq, W), lambda b, i: (b, i, 0)),
                                              seq_spec, seq_spec],
        out_specs=pl.BlockSpec((1, tq, W), lambda b, i: (b, i, 0)),
        out_shape=jax.ShapeDtypeStruct((B, S, W), F32),
        compiler_params=pltpu.CompilerParams(dimension_semantics=("arbitrary", "arbitrary"),
                                             vmem_limit_bytes=VMEM_LIMIT),
        name="attn_prompt",
    )(*lams, hn, qb, kb, vb)


def _attn_sample_body(pt_ref, lq1, lk1, lq2, lk2, hn_ref, q_ref, kn_ref, vn_ref, u_ref, st_ref, kt_hbm, vr_hbm,
                      o_ref, pooled_ref, kbuf, vbuf, sems, qbd_ref, m_ref, l_ref, acc_ref, ext_ref,
                      *, n_chunks, cp, page, t_new, past, lam_init):
    b = pl.program_id(0)
    c = pl.program_id(1)
    nb = pl.num_programs(0)
    step = b * n_chunks + c
    slot = step % 2
    ck = cp * page
    grp = 2 * t_new
    rows = N_HEADS * grp
    W = q_ref.shape[-1]

    def chunk_copies(bb, cc, sl):
        out = []
        for p in range(cp):
            pg = pt_ref[(bb * n_chunks + cc) * cp + p]
            out.append(pltpu.make_async_copy(kt_hbm.at[pg], kbuf.at[sl, :, pl.ds(p * page, page)], sems.at[0, sl]))
            out.append(pltpu.make_async_copy(vr_hbm.at[pg], vbuf.at[sl, pl.ds(p * page * N_HEADS, page * N_HEADS), :],
                                             sems.at[1, sl]))
        return out

    @pl.when(step == 0)
    def _():
        for cpy in chunk_copies(b, c, slot):
            cpy.start()

    @pl.when(step + 1 < nb * n_chunks)
    def _():
        nxt = step + 1
        for cpy in chunk_copies(nxt // n_chunks, nxt % n_chunks, 1 - slot):
            cpy.start()

    r_i = lax.broadcasted_iota(jnp.int32, (rows, 1), 0)
    t_row = r_i % t_new
    h_row = r_i // grp
    slope_row = jnp.exp2(-8.0 * (h_row + 1).astype(F32) / N_HEADS)

    def head_dots(p, v_of_head):
        return jnp.concatenate(
            [jnp.dot(p[h * grp:(h + 1) * grp], v_of_head(h), preferred_element_type=F32) for h in range(N_HEADS)],
            axis=0)

    @pl.when(c == 0)
    def _():
        ext_ref[0:HALO, :] = st_ref[0]
        ext_ref[HALO:HALO + t_new, :] = u_ref[0]
        pos = past + lax.broadcasted_iota(jnp.int32, (t_new, 1), 0)
        inv_cnt = [1.0 / jnp.minimum(pos + 1, w).astype(F32) for w in POOL_WINDOWS]
        pooled = _window_pool(ext_ref, HALO, t_new, inv_cnt)
        for g in range(len(POOL_WINDOWS)):
            pooled_ref[0, :, g * LANES:(g + 1) * LANES] = pooled[g]
        qf = q_ref[0].astype(F32)
        qt = jnp.concatenate([qf] * (2 * N_HEADS), axis=0)
        col_grp = lax.broadcasted_iota(jnp.int32, (rows, W), 1) // (W // (2 * N_HEADS))
        row_grp = lax.broadcasted_iota(jnp.int32, (rows, W), 0) // t_new
        qbd = jnp.where(col_grp == row_grp, qt, 0.0).astype(BF16)
        qbd_ref[...] = qbd
        s = lax.dot_general(qbd, kn_ref[0].astype(BF16), (((1,), (1,)), ((), ())), preferred_element_type=F32)
        kcol = lax.broadcasted_iota(jnp.int32, (rows, t_new), 1)
        dist = t_row - kcol
        s = jnp.where(dist >= 0, s - slope_row * dist.astype(F32), -jnp.inf)
        m0 = jnp.max(s, axis=1, keepdims=True)
        p = jnp.exp(s - m0)
        m_ref[...] = m0
        l_ref[...] = jnp.sum(p, axis=1, keepdims=True)
        vn = vn_ref[0].astype(BF16)
        acc_ref[...] = head_dots(p.astype(BF16), lambda h: vn[:, h * LANES:(h + 1) * LANES])

    for cpy in chunk_copies(b, c, slot):
        cpy.wait()

    s = jnp.dot(qbd_ref[...], kbuf[slot].astype(BF16), preferred_element_type=F32)
    kpos = c * ck + lax.broadcasted_iota(jnp.int32, (rows, ck), 1)
    dist = (past + t_row) - kpos
    s = s - slope_row * dist.astype(F32)
    m_i = m_ref[...]
    m_new = jnp.maximum(m_i, jnp.max(s, axis=1, keepdims=True))
    alpha = jnp.exp(m_i - m_new)
    p = jnp.exp(s - m_new)
    l_new = alpha * l_ref[...] + jnp.sum(p, axis=1, keepdims=True)
    vslot = vbuf.at[slot]
    pv = head_dots(p.astype(BF16), lambda h: vslot[pl.ds(h, ck, stride=N_HEADS), :].astype(BF16))
    acc_new = alpha * acc_ref[...] + pv
    m_ref[...] = m_new
    l_ref[...] = l_new
    acc_ref[...] = acc_new

    @pl.when(c == n_chunks - 1)
    def _():
        lam = _lam_value(lq1, lk1, lq2, lk2, lam_init)
        hn = hn_ref[...]
        for h in range(N_HEADS):
            r1 = slice(h * grp, h * grp + t_new)
            r2 = slice(h * grp + t_new, (h + 1) * grp)
            o = _head_finish(acc_new[r1], l_new[r1], acc_new[r2], l_new[r2], lam, hn, 1.0 - lam_init)
            o_ref[0, :, h * LANES:(h + 1) * LANES] = o


def _attn_sample(page_table, lams, hn, q, k_new, v_new, u, state_pad, cache_kt, cache_vr, *, cp, lam_init):
    nb, t_new, W = q.shape
    n_pages = page_table.shape[1]
    page = cache_kt.shape[2]
    n_chunks = n_pages // cp
    rows = 2 * N_HEADS * t_new
    body = functools.partial(_attn_sample_body, n_chunks=n_chunks, cp=cp, page=page, t_new=t_new,
                             past=n_pages * page, lam_init=lam_init)
    small = lambda a: pl.BlockSpec(a.shape, lambda b, c, pt: (0, 0))
    seq = lambda r: pl.BlockSpec((1, r, W), lambda b, c, pt: (b, 0, 0))
    grid_spec = pltpu.PrefetchScalarGridSpec(
        num_scalar_prefetch=1,
        grid=(nb, n_chunks),
        in_specs=[small(l) for l in lams] + [small(hn), seq(t_new), seq(t_new), seq(t_new), seq(t_new), seq(HALO),
                                              pl.BlockSpec(memory_space=pl.ANY), pl.BlockSpec(memory_space=pl.ANY)],
        out_specs=[seq(t_new), seq(t_new)],
        scratch_shapes=[pltpu.VMEM((2, W, cp * page), F32), pltpu.VMEM((2, cp * page * N_HEADS, LANES), F32),
                        pltpu.SemaphoreType.DMA((2, 2)),
                        pltpu.VMEM((rows, W), BF16), pltpu.VMEM((rows, 1), F32), pltpu.VMEM((rows, 1), F32),
                        pltpu.VMEM((rows, LANES), F32), pltpu.VMEM((HALO + t_new, W), F32)],
    )
    return pl.pallas_call(
        body,
        grid_spec=grid_spec,
        out_shape=[jax.ShapeDtypeStruct((nb, t_new, W), F32), jax.ShapeDtypeStruct((nb, t_new, W), F32)],
        compiler_params=pltpu.CompilerParams(dimension_semantics=("arbitrary", "arbitrary"),
                                             vmem_limit_bytes=VMEM_LIMIT),
        name="attn_sample",
    )(page_table.reshape(-1), *lams, hn, q, k_new, v_new, u, state_pad, cache_kt, cache_vr)


def _merge_body(h_ref, pooled_ref, zp_ref, o_ref, za_ref, p_ref, wpool_ref, spool_ref, wout_ref, wgate_ref,
                wproj_ref, nf_ref, y_ref):
    d_pool = pooled_ref.shape[-1]
    y_pool = jnp.concatenate(
        [jnp.dot(pooled_ref[:, g * LANES:(g + 1) * LANES].astype(BF16), wpool_ref[g], preferred_element_type=F32)
         for g in range(len(POOL_WINDOWS))], axis=-1) * spool_ref[...]
    ya = (y_pool * jax.nn.silu(zp_ref[...])).astype(BF16)
    yb = (o_ref[...] * jax.nn.silu(za_ref[...])).astype(BF16)
    h1 = (h_ref[...] + jnp.dot(ya, wout_ref[0:d_pool, :], preferred_element_type=F32)
          + jnp.dot(yb, wout_ref[d_pool:, :], preferred_element_type=F32))
    gate = jax.nn.sigmoid(jnp.dot(h1.astype(BF16), wgate_ref[...], preferred_element_type=F32))
    h2 = h1 + gate * jnp.dot(p_ref[...].astype(BF16), wproj_ref[...], preferred_element_type=F32)
    y_ref[...] = _rmsnorm(h2, nf_ref[...])


def _merge(h, pooled, zp, o, za, p, w_pool, s_pool, w_out, w_gate, w_proj, norm_final, *, tm):
    T, D = h.shape
    row = lambda a: pl.BlockSpec((tm, a.shape[1]), lambda i: (i, 0))
    full = lambda a: pl.BlockSpec(a.shape, lambda i: (0,) * a.ndim)
    acts = (h, pooled, zp, o, za, p)
    consts = (w_pool, s_pool, w_out, w_gate, w_proj, norm_final)
    return pl.pallas_call(
        _merge_body,
        grid=(T // tm,),
        in_specs=[row(a) for a in acts] + [full(a) for a in consts],
        out_specs=pl.BlockSpec((tm, D), lambda i: (i, 0)),
        out_shape=jax.ShapeDtypeStruct((T, D), F32),
        compiler_params=pltpu.CompilerParams(dimension_semantics=("arbitrary",), vmem_limit_bytes=VMEM_LIMIT),
        name="merge",
    )(*acts, *consts)


def kernel(x_prompt, x_sample, cache_k, cache_v, state_pool, page_table, p_prompt, p_sample, norm_in, w_in, w_pool,
           s_pool, lambda_q1, lambda_k1, lambda_q2, lambda_k2, head_norm, w_out, w_ple_gate, w_ple_proj, norm_final):
    B, S, D = x_prompt.shape
    NB, T_NEW, _ = x_sample.shape
    depth = w_in.shape[0]
    assert depth == 1, "single-layer step"
    d_pool = state_pool.shape[-1]
    n_phys, page = cache_k.shape[1], cache_k.shape[2]
    hp = x_prompt.reshape(B * S, D)
    hs = x_sample.reshape(NB * T_NEW, D)
    outs = {}
    for i in range(depth):
        lam_init = _lambda_init(i)
        g_in = norm_in[i].reshape(1, D)
        w_in_b = w_in[i].astype(BF16)
        lams = tuple(a[i].reshape(1, -1) for a in (lambda_q1, lambda_k1, lambda_q2, lambda_k2))
        hn = head_norm[i].reshape(1, -1)
        consts = (w_pool[i].astype(BF16), s_pool[i].reshape(1, -1), w_out[i].astype(BF16),
                  w_ple_gate[i].astype(BF16), w_ple_proj[i].astype(BF16), norm_final.reshape(1, D))

        pooled, tail, zp, qb, k, v, za, kb, vb = _inproj_prompt(hp.reshape(B, S, D), g_in, w_in_b, tm=512)
        o = _attn_prompt(lams, hn, qb.reshape(B, S, -1), kb.reshape(B, S, -1), vb.reshape(B, S, -1),
                         tq=256, lam_init=lam_init)
        y_prompt = _merge(hp, pooled.reshape(B * S, -1), zp, o.reshape(B * S, -1), za,
                          p_prompt[i].reshape(B * S, -1), *consts, tm=512)
        outs["k_p"] = k.reshape(B, S, N_HEADS, 2, -1)
        outs["v_p"] = v.reshape(B, S, N_HEADS, -1)
        outs["pool_p"] = tail[:, HALO - POOL_STATE_LEN:]

        u, zp, qb, k, v, za = _inproj_sample(hs, g_in, w_in_b, tm=256)
        seq = lambda a: a.reshape(NB, T_NEW, -1)
        state_pad = jnp.pad(state_pool[i], ((0, 0), (HALO - POOL_STATE_LEN, 0), (0, 0)))
        o, pooled = _attn_sample(page_table, lams, hn, seq(qb), seq(k), seq(v), seq(u), state_pad,
                                 jnp.transpose(cache_k[i], (0, 2, 3, 4, 1)).reshape(n_phys, -1, page),
                                 cache_v[i].reshape(n_phys, page * N_HEADS, -1),
                                 cp=16, lam_init=lam_init)
        y_sample = _merge(hs, pooled.reshape(NB * T_NEW, -1), zp, o.reshape(NB * T_NEW, -1), za,
                          p_sample[i].reshape(NB * T_NEW, -1), *consts, tm=256)
        outs["k_s"] = k.reshape(NB, T_NEW, N_HEADS, 2, -1)
        outs["v_s"] = v.reshape(NB, T_NEW, N_HEADS, -1)
        outs["pool_s"] = jnp.concatenate([state_pool[i], seq(u)], axis=1)[:, -POOL_STATE_LEN:]
    return (y_prompt.reshape(B, S, D), y_sample.reshape(NB, T_NEW, D),
            outs["k_p"][None], outs["v_p"][None], outs["pool_p"][None],
            outs["k_s"][None], outs["v_s"][None], outs["pool_s"][None])
```

```python
import functools
import math

import jax
import jax.numpy as jnp
from jax import lax
from jax.experimental import pallas as pl
from jax.experimental.pallas import tpu as pltpu

F32 = jnp.float32
BF16 = jnp.bfloat16

EPS = 1e-6
POOL_WINDOWS = (2, 4, 8, 16)
POOL_STATE_LEN = max(POOL_WINDOWS) - 1
HALO = 16
N_HEADS = 4
LANES = 128
VMEM_LIMIT = 56 * 1024 * 1024
LOG2E = math.log2(math.e)
TM_PROMPT = 512
TQ_PROMPT = 256
TM_SAMPLE = 256
PAGES_PER_CHUNK = 16


def _lambda_init(layer):
    return 0.8 - 0.6 * math.exp(-0.3 * layer)


def _alibi_slope(h):
    return 2.0 ** (-8.0 * (h + 1) / N_HEADS)


def _rmsnorm(xf, g):
    return xf * lax.rsqrt(jnp.mean(xf * xf, axis=-1, keepdims=True) + EPS) * g


def _lam_value(lq1, lk1, lq2, lk2, lam_init):
    a = jnp.sum(lq1[...] * lk1[...], axis=-1, keepdims=True)
    b = jnp.sum(lq2[...] * lk2[...], axis=-1, keepdims=True)
    return jnp.exp(a) - jnp.exp(b) + lam_init


def _window_pool(ext_ref, base, rows, inv_cnt):
    outs = []
    for g, w in enumerate(POOL_WINDOWS):
        cols = slice(g * LANES, (g + 1) * LANES)
        cur = ext_ref[base:base + rows, cols]
        acc = cur
        for j in range(1, w):
            acc = acc + ext_ref[base - j:base - j + rows, cols]
        outs.append(acc * inv_cnt[g] - cur)
    return outs


def _inproj_body(x_ref, g_ref, w_ref, *refs, tm, tk, d_split, q_scale, prompt):
    if prompt:
        (wkt_ref, pooled_ref, tail_ref, zp_ref, q_ref, kt_ref, vrow_ref, za_ref, ktb_ref, vb_ref, ext_ref) = refs
    else:
        (u_ref, zp_ref, q_ref, k_ref, v_ref, za_ref) = refs
    a = _rmsnorm(x_ref[...], g_ref[...]).astype(BF16)

    def proj(c):
        return jnp.dot(a, w_ref[:, c * d_split:(c + 1) * d_split], preferred_element_type=F32)

    u = proj(0)
    zp_ref[...] = proj(1)
    q_ref[...] = (proj(2) * q_scale).astype(BF16)
    v = proj(4)
    za_ref[...] = proj(5)
    if not prompt:
        u_ref[...] = u
        k_ref[...] = proj(3)
        v_ref[...] = v
        return
    kt = lax.dot_general(wkt_ref[...], a, (((1,), (1,)), ((), ())), preferred_element_type=F32)
    kt_ref[0] = kt
    for c in range(tm // tk):
        ktb_ref[0, c] = kt[:, c * tk:(c + 1) * tk].astype(BF16)
    for h in range(N_HEADS):
        vrow_ref[pl.ds(h, tm, stride=N_HEADS), :] = v[:, h * LANES:(h + 1) * LANES]
    vb_ref[...] = v.astype(BF16)
    i = pl.program_id(1)

    @pl.when(i == 0)
    def _():
        ext_ref[0:HALO, :] = jnp.zeros((HALO, d_split), F32)

    ext_ref[HALO:HALO + tm, :] = u
    pos = i * tm + lax.broadcasted_iota(jnp.int32, (tm, 1), 0)
    inv_cnt = [1.0 / jnp.minimum(pos + 1, w).astype(F32) for w in POOL_WINDOWS]
    pooled = _window_pool(ext_ref, HALO, tm, inv_cnt)
    for g in range(len(POOL_WINDOWS)):
        pooled_ref[:, g * LANES:(g + 1) * LANES] = pooled[g]
    tail = ext_ref[tm:tm + HALO, :]
    tail_ref[0] = tail
    ext_ref[0:HALO, :] = tail


def _inproj_prompt(x, g, w_in, wkt, *, tm, tk, q_scale):
    B, S, D = x.shape
    d_split = w_in.shape[1] // 6
    nt = S // tm
    flat_spec = pl.BlockSpec((tm, d_split), lambda b, i: (b * nt + i, 0))
    f32_tok = jax.ShapeDtypeStruct((B * S, d_split), F32)
    bf_tok = jax.ShapeDtypeStruct((B * S, d_split), BF16)
    const = lambda a: pl.BlockSpec(a.shape, lambda b, i: (0,) * a.ndim)
    body = functools.partial(_inproj_body, tm=tm, tk=tk, d_split=d_split, q_scale=q_scale, prompt=True)
    return pl.pallas_call(
        body,
        grid=(B, nt),
        in_specs=[pl.BlockSpec((tm, D), lambda b, i: (b * nt + i, 0)), const(g), const(w_in), const(wkt)],
        out_specs=[flat_spec,
                   pl.BlockSpec((1, HALO, d_split), lambda b, i: (b, 0, 0)),
                   flat_spec, flat_spec,
                   pl.BlockSpec((1, d_split, tm), lambda b, i: (b, 0, i)),
                   pl.BlockSpec((tm * N_HEADS, LANES), lambda b, i: (b * nt + i, 0)),
                   flat_spec,
                   pl.BlockSpec((1, tm // tk, d_split, tk), lambda b, i: (b, i, 0, 0)),
                   flat_spec],
        out_shape=[f32_tok,
                   jax.ShapeDtypeStruct((B, HALO, d_split), F32),
                   f32_tok, bf_tok,
                   jax.ShapeDtypeStruct((B, d_split, S), F32),
                   jax.ShapeDtypeStruct((B * S * N_HEADS, LANES), F32),
                   f32_tok,
                   jax.ShapeDtypeStruct((B, S // tk, d_split, tk), BF16),
                   bf_tok],
        scratch_shapes=[pltpu.VMEM((HALO + tm, d_split), F32)],
        compiler_params=pltpu.CompilerParams(dimension_semantics=("arbitrary", "arbitrary"),
                                             vmem_limit_bytes=VMEM_LIMIT),
        name="inproj_prompt",
    )(x.reshape(B * S, D), g, w_in, wkt)


def _inproj_sample(x2d, g, w_in, *, tm, q_scale):
    T, D = x2d.shape
    d_split = w_in.shape[1] // 6
    spec = pl.BlockSpec((tm, d_split), lambda i: (i, 0))
    f32_tok = jax.ShapeDtypeStruct((T, d_split), F32)
    body = functools.partial(_inproj_body, tm=tm, tk=None, d_split=d_split, q_scale=q_scale, prompt=False)
    return pl.pallas_call(
        body,
        grid=(T // tm,),
        in_specs=[pl.BlockSpec((tm, D), lambda i: (i, 0)),
                  pl.BlockSpec((1, D), lambda i: (0, 0)),
                  pl.BlockSpec(w_in.shape, lambda i: (0, 0))],
        out_specs=[spec] * 6,
        out_shape=[f32_tok, f32_tok, jax.ShapeDtypeStruct((T, d_split), BF16), f32_tok, f32_tok, f32_tok],
        compiler_params=pltpu.CompilerParams(dimension_semantics=("arbitrary",), vmem_limit_bytes=VMEM_LIMIT),
        name="inproj_sample",
    )(x2d, g, w_in)


def _head_finish(o1, l1, o2, l2, lam, hn, out_scale):
    o = o1 / l1 - lam * (o2 / l2)
    return _rmsnorm(o, hn) * out_scale


def _attn_prompt_body(lq1, lk1, lq2, lk2, hn_ref, q_ref, kt_ref, v_ref, o_ref, q2_ref, m_ref, l_ref, acc_ref,
                      *, tq, lam_init):
    i = pl.program_id(1)
    rows = 2 * tq
    half = LANES // 2
    lane = lax.broadcasted_iota(jnp.int32, (tq, LANES), 1)
    for h in range(N_HEADS):
        qh = q_ref[0, :, h * LANES:(h + 1) * LANES]
        zero = jnp.zeros_like(qh)
        q2_ref[h] = jnp.concatenate([jnp.where(lane < half, qh, zero), jnp.where(lane >= half, qh, zero)], axis=0)
    m_ref[...] = jnp.full(m_ref.shape, -jnp.inf, F32)
    l_ref[...] = jnp.zeros(l_ref.shape, F32)
    acc_ref[...] = jnp.zeros(acc_ref.shape, F32)
    col = lax.broadcasted_iota(jnp.int32, (1, tq), 1).astype(F32)

    def block(j, masked):
        kpos = col + ((j - i) * tq).astype(F32)
        for h in range(N_HEADS):
            s = jnp.dot(q2_ref[h], kt_ref[0, j, h * LANES:(h + 1) * LANES, :], preferred_element_type=F32)
            s = s + (_alibi_slope(h) * LOG2E) * kpos
            if masked:
                r_i = lax.broadcasted_iota(jnp.int32, (rows, tq), 0)
                c_i = lax.broadcasted_iota(jnp.int32, (rows, tq), 1)
                s = jnp.where(c_i <= jnp.where(r_i >= tq, r_i - tq, r_i), s, -jnp.inf)
            m_old = m_ref[h]
            m_new = jnp.maximum(m_old, jnp.max(s, axis=1, keepdims=True))
            alpha = jnp.exp2(m_old - m_new)
            ps = [jnp.exp2(s[:, t * LANES:(t + 1) * LANES] - m_new) for t in range(tq // LANES)]
            l_ref[h] = alpha * l_ref[h] + functools.reduce(lambda x, y: x + y, ps)
            vb = v_ref[0, pl.ds(pl.multiple_of(j * tq, tq), tq), h * LANES:(h + 1) * LANES]
            p = jnp.concatenate(ps, axis=1).astype(BF16)
            acc_ref[h] = alpha * acc_ref[h] + jnp.dot(p, vb, preferred_element_type=F32)
            m_ref[h] = m_new

    def full_block(j, carry):
        block(j, False)
        return carry

    lax.fori_loop(0, i, full_block, 0)
    block(i, True)
    lam = _lam_value(lq1, lk1, lq2, lk2, lam_init)
    hn = hn_ref[...]
    for h in range(N_HEADS):
        l_h = jnp.sum(l_ref[h], axis=1, keepdims=True)
        o = _head_finish(acc_ref[h, :tq], l_h[:tq], acc_ref[h, tq:], l_h[tq:], lam, hn, 1.0 - lam_init)
        o_ref[0, :, h * LANES:(h + 1) * LANES] = o


def _attn_prompt(lams, hn, qb, ktb, vb, *, tq, lam_init):
    B, S, W = qb.shape
    body = functools.partial(_attn_prompt_body, tq=tq, lam_init=lam_init)
    small = lambda a: pl.BlockSpec(a.shape, lambda b, i: (0, 0))
    rows = 2 * tq
    return pl.pallas_call(
        body,
        grid=(B, S // tq),
        in_specs=[small(l) for l in lams] + [small(hn), pl.BlockSpec((1, tq, W), lambda b, i: (b, i, 0)),
                                              pl.BlockSpec((1,) + ktb.shape[1:], lambda b, i: (b, 0, 0, 0)),
                                              pl.BlockSpec((1, S, W), lambda b, i: (b, 0, 0))],
        out_specs=pl.BlockSpec((1, tq, W), lambda b, i: (b, i, 0)),
        out_shape=jax.ShapeDtypeStruct((B, S, W), F32),
        scratch_shapes=[pltpu.VMEM((N_HEADS, rows, LANES), BF16), pltpu.VMEM((N_HEADS, rows, LANES), F32),
                        pltpu.VMEM((N_HEADS, rows, LANES), F32), pltpu.VMEM((N_HEADS, rows, LANES), F32)],
        compiler_params=pltpu.CompilerParams(dimension_semantics=("arbitrary", "arbitrary"),
                                             vmem_limit_bytes=VMEM_LIMIT),
        name="attn_prompt",
    )(*lams, hn, qb, ktb, vb)


def _attn_sample_body(pt_ref, lq1, lk1, lq2, lk2, hn_ref, q_ref, kn_ref, vn_ref, u_ref, st_ref, kt_hbm, vr_hbm,
                      o_ref, pooled_ref, kbuf, vbuf, sems, qbd_ref, m_ref, l_ref, acc_ref, ext_ref,
                      *, n_chunks, cp, page, t_new, past, lam_init):
    b = pl.program_id(0)
    c = pl.program_id(1)
    nb = pl.num_programs(0)
    step = b * n_chunks + c
    slot = step % 2
    ck = cp * page
    grp = 2 * t_new
    rows = N_HEADS * grp
    W = q_ref.shape[-1]

    def chunk_copies(bb, cc, sl):
        out = []
        for p in range(cp):
            pg = pt_ref[(bb * n_chunks + cc) * cp + p]
            out.append(pltpu.make_async_copy(kt_hbm.at[pg], kbuf.at[sl, :, pl.ds(p * page, page)], sems.at[0, sl]))
            out.append(pltpu.make_async_copy(vr_hbm.at[pg], vbuf.at[sl, pl.ds(p * page * N_HEADS, page * N_HEADS), :],
                                             sems.at[1, sl]))
        return out

    @pl.when(step == 0)
    def _():
        for cpy in chunk_copies(b, c, slot):
            cpy.start()

    @pl.when(step + 1 < nb * n_chunks)
    def _():
        nxt = step + 1
        for cpy in chunk_copies(nxt // n_chunks, nxt % n_chunks, 1 - slot):
            cpy.start()

    r_i = lax.broadcasted_iota(jnp.int32, (rows, 1), 0)
    t_row = r_i % t_new
    h_row = r_i // grp
    slope_row = jnp.exp2(-8.0 * (h_row + 1).astype(F32) / N_HEADS)

    def head_dots(p, v_of_head):
        return jnp.concatenate(
            [jnp.dot(p[h * grp:(h + 1) * grp], v_of_head(h), preferred_element_type=F32) for h in range(N_HEADS)],
            axis=0)

    @pl.when(c == 0)
    def _():
        ext_ref[0:HALO, :] = st_ref[0]
        ext_ref[HALO:HALO + t_new, :] = u_ref[0]
        pos = past + lax.broadcasted_iota(jnp.int32, (t_new, 1), 0)
        inv_cnt = [1.0 / jnp.minimum(pos + 1, w).astype(F32) for w in POOL_WINDOWS]
        pooled = _window_pool(ext_ref, HALO, t_new, inv_cnt)
        for g in range(len(POOL_WINDOWS)):
            pooled_ref[0, :, g * LANES:(g + 1) * LANES] = pooled[g]
        qf = q_ref[0].astype(F32)
        qt = jnp.concatenate([qf] * (2 * N_HEADS), axis=0)
        col_grp = lax.broadcasted_iota(jnp.int32, (rows, W), 1) // (W // (2 * N_HEADS))
        row_grp = lax.broadcasted_iota(jnp.int32, (rows, W), 0) // t_new
        qbd = jnp.where(col_grp == row_grp, qt, 0.0).astype(BF16)
        qbd_ref[...] = qbd
        s = lax.dot_general(qbd, kn_ref[0].astype(BF16), (((1,), (1,)), ((), ())), preferred_element_type=F32)
        kcol = lax.broadcasted_iota(jnp.int32, (rows, t_new), 1)
        dist = t_row - kcol
        s = jnp.where(dist >= 0, s - slope_row * dist.astype(F32), -jnp.inf)
        m0 = jnp.max(s, axis=1, keepdims=True)
        p = jnp.exp(s - m0)
        m_ref[...] = m0
        l_ref[...] = jnp.sum(p, axis=1, keepdims=True)
        vn = vn_ref[0].astype(BF16)
        acc_ref[...] = head_dots(p.astype(BF16), lambda h: vn[:, h * LANES:(h + 1) * LANES])

    for cpy in chunk_copies(b, c, slot):
        cpy.wait()

    s = jnp.dot(qbd_ref[...], kbuf[slot].astype(BF16), preferred_element_type=F32)
    kpos = c * ck + lax.broadcasted_iota(jnp.int32, (rows, ck), 1)
    dist = (past + t_row) - kpos
    s = s - slope_row * dist.astype(F32)
    m_i = m_ref[...]
    m_new = jnp.maximum(m_i, jnp.max(s, axis=1, keepdims=True))
    alpha = jnp.exp(m_i - m_new)
    p = jnp.exp(s - m_new)
    l_new = alpha * l_ref[...] + jnp.sum(p, axis=1, keepdims=True)
    vslot = vbuf.at[slot]
    pv = head_dots(p.astype(BF16), lambda h: vslot[pl.ds(h, ck, stride=N_HEADS), :].astype(BF16))
    acc_new = alpha * acc_ref[...] + pv
    m_ref[...] = m_new
    l_ref[...] = l_new
    acc_ref[...] = acc_new

    @pl.when(c == n_chunks - 1)
    def _():
        lam = _lam_value(lq1, lk1, lq2, lk2, lam_init)
        hn = hn_ref[...]
        for h in range(N_HEADS):
            r1 = slice(h * grp, h * grp + t_new)
            r2 = slice(h * grp + t_new, (h + 1) * grp)
            o = _head_finish(acc_new[r1], l_new[r1], acc_new[r2], l_new[r2], lam, hn, 1.0 - lam_init)
            o_ref[0, :, h * LANES:(h + 1) * LANES] = o


def _attn_sample(page_table, lams, hn, q, k_new, v_new, u, state_pad, cache_kt, cache_vr, *, cp, lam_init):
    nb, t_new, W = q.shape
    n_pages = page_table.shape[1]
    page = cache_kt.shape[2]
    n_chunks = n_pages // cp
    rows = 2 * N_HEADS * t_new
    body = functools.partial(_attn_sample_body, n_chunks=n_chunks, cp=cp, page=page, t_new=t_new,
                             past=n_pages * page, lam_init=lam_init)
    small = lambda a: pl.BlockSpec(a.shape, lambda b, c, pt: (0, 0))
    seq = lambda r: pl.BlockSpec((1, r, W), lambda b, c, pt: (b, 0, 0))
    grid_spec = pltpu.PrefetchScalarGridSpec(
        num_scalar_prefetch=1,
        grid=(nb, n_chunks),
        in_specs=[small(l) for l in lams] + [small(hn), seq(t_new), seq(t_new), seq(t_new), seq(t_new), seq(HALO),
                                              pl.BlockSpec(memory_space=pl.ANY), pl.BlockSpec(memory_space=pl.ANY)],
        out_specs=[seq(t_new), seq(t_new)],
        scratch_shapes=[pltpu.VMEM((2, W, cp * page), F32), pltpu.VMEM((2, cp * page * N_HEADS, LANES), F32),
                        pltpu.SemaphoreType.DMA((2, 2)),
                        pltpu.VMEM((rows, W), BF16), pltpu.VMEM((rows, 1), F32), pltpu.VMEM((rows, 1), F32),
                        pltpu.VMEM((rows, LANES), F32), pltpu.VMEM((HALO + t_new, W), F32)],
    )
    return pl.pallas_call(
        body,
        grid_spec=grid_spec,
        out_shape=[jax.ShapeDtypeStruct((nb, t_new, W), F32), jax.ShapeDtypeStruct((nb, t_new, W), F32)],
        compiler_params=pltpu.CompilerParams(dimension_semantics=("arbitrary", "arbitrary"),
                                             vmem_limit_bytes=VMEM_LIMIT),
        name="attn_sample",
    )(page_table.reshape(-1), *lams, hn, q, k_new, v_new, u, state_pad, cache_kt, cache_vr)


def _merge_body(h_ref, pooled_ref, zp_ref, o_ref, za_ref, p_ref, wpool_ref, spool_ref, wout_ref, wgate_ref,
                wproj_ref, nf_ref, y_ref):
    d_pool = pooled_ref.shape[-1]
    y_pool = jnp.concatenate(
        [jnp.dot(pooled_ref[:, g * LANES:(g + 1) * LANES].astype(BF16), wpool_ref[g], preferred_element_type=F32)
         for g in range(len(POOL_WINDOWS))], axis=-1) * spool_ref[...]
    ya = (y_pool * jax.nn.silu(zp_ref[...])).astype(BF16)
    yb = (o_ref[...] * jax.nn.silu(za_ref[...])).astype(BF16)
    h1 = (h_ref[...] + jnp.dot(ya, wout_ref[0:d_pool, :], preferred_element_type=F32)
          + jnp.dot(yb, wout_ref[d_pool:, :], preferred_element_type=F32))
    gate = jax.nn.sigmoid(jnp.dot(h1.astype(BF16), wgate_ref[...], preferred_element_type=F32))
    h2 = h1 + gate * jnp.dot(p_ref[...].astype(BF16), wproj_ref[...], preferred_element_type=F32)
    y_ref[...] = _rmsnorm(h2, nf_ref[...])


def _merge(h, pooled, zp, o, za, p, w_pool, s_pool, w_out, w_gate, w_proj, norm_final, *, tm):
    T, D = h.shape
    row = lambda a: pl.BlockSpec((tm, a.shape[1]), lambda i: (i, 0))
    full = lambda a: pl.BlockSpec(a.shape, lambda i: (0,) * a.ndim)
    acts = (h, pooled, zp, o, za, p)
    consts = (w_pool, s_pool, w_out, w_gate, w_proj, norm_final)
    return pl.pallas_call(
        _merge_body,
        grid=(T // tm,),
        in_specs=[row(a) for a in acts] + [full(a) for a in consts],
        out_specs=pl.BlockSpec((tm, D), lambda i: (i, 0)),
        out_shape=jax.ShapeDtypeStruct((T, D), F32),
        compiler_params=pltpu.CompilerParams(dimension_semantics=("arbitrary",), vmem_limit_bytes=VMEM_LIMIT),
        name="merge",
    )(*acts, *consts)


def kernel(x_prompt, x_sample, cache_k, cache_v, state_pool, page_table, p_prompt, p_sample, norm_in, w_in, w_pool,
           s_pool, lambda_q1, lambda_k1, lambda_q2, lambda_k2, head_norm, w_out, w_ple_gate, w_ple_proj, norm_final):
    B, S, D = x_prompt.shape
    NB, T_NEW, _ = x_sample.shape
    depth = w_in.shape[0]
    assert depth == 1, "single-layer step"
    n_phys, page = cache_k.shape[1], cache_k.shape[2]
    hp = x_prompt.reshape(B * S, D)
    hs = x_sample.reshape(NB * T_NEW, D)
    outs = {}
    for i in range(depth):
        lam_init = _lambda_init(i)
        g_in = norm_in[i].reshape(1, D)
        w_in_b = w_in[i].astype(BF16)
        lams = tuple(a[i].reshape(1, -1) for a in (lambda_q1, lambda_k1, lambda_q2, lambda_k2))
        hn = head_norm[i].reshape(1, -1)
        consts = (w_pool[i].astype(BF16), s_pool[i].reshape(1, -1), w_out[i].astype(BF16),
                  w_ple_gate[i].astype(BF16), w_ple_proj[i].astype(BF16), norm_final.reshape(1, D))
        d_split = w_in.shape[-1] // 6
        attn_scale = (d_split // N_HEADS // 2) ** -0.5

        wkt = w_in[i][:, 3 * d_split:4 * d_split].T.astype(BF16)
        pooled, tail, zp, qb, kt, vrow, za, ktb, vb = _inproj_prompt(
            hp.reshape(B, S, D), g_in, w_in_b, wkt, tm=TM_PROMPT, tk=TQ_PROMPT, q_scale=attn_scale * LOG2E)
        o = _attn_prompt(lams, hn, qb.reshape(B, S, -1), ktb, vb.reshape(B, S, -1), tq=TQ_PROMPT, lam_init=lam_init)
        y_prompt = _merge(hp, pooled, zp, o.reshape(B * S, -1), za, p_prompt[i].reshape(B * S, -1), *consts,
                          tm=TM_PROMPT)
        outs["k_p"] = jnp.transpose(kt.reshape(B, N_HEADS, 2, -1, S), (0, 4, 1, 2, 3))
        outs["v_p"] = vrow.reshape(B, S, N_HEADS, -1)
        outs["pool_p"] = tail[:, HALO - POOL_STATE_LEN:]

        u, zp, qb, k, v, za = _inproj_sample(hs, g_in, w_in_b, tm=TM_SAMPLE, q_scale=attn_scale)
        seq = lambda a: a.reshape(NB, T_NEW, -1)
        state_pad = jnp.pad(state_pool[i], ((0, 0), (HALO - POOL_STATE_LEN, 0), (0, 0)))
        o, pooled = _attn_sample(page_table, lams, hn, seq(qb), seq(k), seq(v), seq(u), state_pad,
                                 jnp.transpose(cache_k[i], (0, 2, 3, 4, 1)).reshape(n_phys, -1, page),
                                 cache_v[i].reshape(n_phys, page * N_HEADS, -1),
                                 cp=PAGES_PER_CHUNK, lam_init=lam_init)
        y_sample = _merge(hs, pooled.reshape(NB * T_NEW, -1), zp, o.reshape(NB * T_NEW, -1), za,
                          p_sample[i].reshape(NB * T_NEW, -1), *consts, tm=TM_SAMPLE)
        outs["k_s"] = k.reshape(NB, T_NEW, N_HEADS, 2, -1)
        outs["v_s"] = v.reshape(NB, T_NEW, N_HEADS, -1)
        outs["pool_s"] = jnp.concatenate([state_pool[i], seq(u)], axis=1)[:, -POOL_STATE_LEN:]
    return (y_prompt.reshape(B, S, D), y_sample.reshape(NB, T_NEW, D),
            outs["k_p"][None], outs["v_p"][None], outs["pool_p"][None],
            outs["k_s"][None], outs["v_s"][None], outs["pool_s"][None])
```

```python
import functools
import math

import jax
import jax.numpy as jnp
from jax import lax
from jax.experimental import pallas as pl
from jax.experimental.pallas import tpu as pltpu

F32 = jnp.float32
BF16 = jnp.bfloat16

EPS = 1e-6
POOL_WINDOWS = (2, 4, 8, 16)
POOL_STATE_LEN = max(POOL_WINDOWS) - 1
HALO = 16
N_HEADS = 4
LANES = 128
VMEM_LIMIT = 56 * 1024 * 1024
LOG2E = math.log2(math.e)
TM_PROMPT = 512
TQ_PROMPT = 256
TM_SAMPLE = 256
PAGES_PER_CHUNK = 16


def _lambda_init(layer):
    return 0.8 - 0.6 * math.exp(-0.3 * layer)


def _alibi_slope(h):
    return 2.0 ** (-8.0 * (h + 1) / N_HEADS)


def _rmsnorm(xf, g):
    return xf * lax.rsqrt(jnp.mean(xf * xf, axis=-1, keepdims=True) + EPS) * g


def _lam_value(lq1, lk1, lq2, lk2, lam_init):
    a = jnp.sum(lq1[...] * lk1[...], axis=-1, keepdims=True)
    b = jnp.sum(lq2[...] * lk2[...], axis=-1, keepdims=True)
    return jnp.exp(a) - jnp.exp(b) + lam_init


def _window_pool(ext_ref, base, rows, inv_cnt):
    outs = []
    for g, w in enumerate(POOL_WINDOWS):
        cols = slice(g * LANES, (g + 1) * LANES)
        cur = ext_ref[base:base + rows, cols]
        acc = cur
        for j in range(1, w):
            acc = acc + ext_ref[base - j:base - j + rows, cols]
        outs.append(acc * inv_cnt[g] - cur)
    return outs


def _head_finish(o1, l1, o2, l2, lam, hn, out_scale):
    o = o1 / l1 - lam * (o2 / l2)
    return _rmsnorm(o, hn) * out_scale


def _project(x_ref, g_ref, w_ref, d_split, q_scale):
    a = _rmsnorm(x_ref[...], g_ref[...]).astype(BF16)

    def proj(c):
        return jnp.dot(a, w_ref[:, c * d_split:(c + 1) * d_split], preferred_element_type=F32)

    return a, proj(0), proj(1), (proj(2) * q_scale).astype(BF16), proj


def _inproj_sample_body(x_ref, g_ref, w_ref, u_ref, zp_ref, q_ref, k_ref, v_ref, za_ref, *, d_split, q_scale):
    _, u, zp, q, proj = _project(x_ref, g_ref, w_ref, d_split, q_scale)
    u_ref[...] = u
    zp_ref[...] = zp
    q_ref[...] = q
    k_ref[...] = proj(3)
    v_ref[...] = proj(4)
    za_ref[...] = proj(5)


def _inproj_sample(x2d, g, w_in, *, tm, q_scale):
    T, D = x2d.shape
    d_split = w_in.shape[1] // 6
    spec = pl.BlockSpec((tm, d_split), lambda i: (i, 0))
    f32_tok = jax.ShapeDtypeStruct((T, d_split), F32)
    body = functools.partial(_inproj_sample_body, d_split=d_split, q_scale=q_scale)
    return pl.pallas_call(
        body,
        grid=(T // tm,),
        in_specs=[pl.BlockSpec((tm, D), lambda i: (i, 0)),
                  pl.BlockSpec((1, D), lambda i: (0, 0)),
                  pl.BlockSpec(w_in.shape, lambda i: (0, 0))],
        out_specs=[spec] * 6,
        out_shape=[f32_tok, f32_tok, jax.ShapeDtypeStruct((T, d_split), BF16), f32_tok, f32_tok, f32_tok],
        compiler_params=pltpu.CompilerParams(dimension_semantics=("arbitrary",), vmem_limit_bytes=VMEM_LIMIT),
        name="inproj_sample",
    )(x2d, g, w_in)


def _paged_attention_seq(seq, n_seq, pt_ref, lam, hn, qs_ref, kn_ref, vn_ref, kt_hbm, vr_hbm, kbuf, vbuf, sems,
                         *, n_chunks, cp, page, t_new, past, lam_init, between):
    ck = cp * page
    grp = 2 * t_new
    rows = N_HEADS * grp
    W = qs_ref.shape[-1]

    def chunk_copies(sq, c):
        sl = c % 2
        out = []
        for p in range(cp):
            pg = pt_ref[(sq * n_chunks + c) * cp + p]
            out.append(pltpu.make_async_copy(kt_hbm.at[pg], kbuf.at[sl, :, pl.ds(p * page, page)], sems.at[0, sl]))
            out.append(pltpu.make_async_copy(vr_hbm.at[pg], vbuf.at[sl, pl.ds(p * page * N_HEADS, page * N_HEADS), :],
                                             sems.at[1, sl]))
        return out

    @pl.when(seq == 0)
    def _():
        for cpy in chunk_copies(seq, 0):
            cpy.start()

    r_i = lax.broadcasted_iota(jnp.int32, (rows, 1), 0)
    t_row = r_i % t_new
    h_row = r_i // grp
    slope_row = jnp.exp2(-8.0 * (h_row + 1).astype(F32) / N_HEADS)

    def head_dots(p, v_of_head):
        return jnp.concatenate(
            [jnp.dot(p[h * grp:(h + 1) * grp], v_of_head(h), preferred_element_type=F32) for h in range(N_HEADS)],
            axis=0)

    qf = qs_ref[0].astype(F32)
    qt = jnp.concatenate([qf] * (2 * N_HEADS), axis=0)
    col_grp = lax.broadcasted_iota(jnp.int32, (rows, W), 1) // (W // (2 * N_HEADS))
    row_grp = lax.broadcasted_iota(jnp.int32, (rows, W), 0) // t_new
    qbd = jnp.where(col_grp == row_grp, qt, 0.0).astype(BF16)
    s = lax.dot_general(qbd, kn_ref[0].astype(BF16), (((1,), (1,)), ((), ())), preferred_element_type=F32)
    kcol = lax.broadcasted_iota(jnp.int32, (rows, t_new), 1)
    dist = t_row - kcol
    s = jnp.where(dist >= 0, s - slope_row * dist.astype(F32), -jnp.inf)
    m_i = jnp.max(s, axis=1, keepdims=True)
    p = jnp.exp(s - m_i)
    l_i = jnp.sum(p, axis=1, keepdims=True)
    vn = vn_ref[0].astype(BF16)
    acc = head_dots(p.astype(BF16), lambda h: vn[:, h * LANES:(h + 1) * LANES])

    for c in range(n_chunks):
        if c + 1 < n_chunks:
            for cpy in chunk_copies(seq, c + 1):
                cpy.start()
        else:
            @pl.when(seq + 1 < n_seq)
            def _():
                for cpy in chunk_copies(seq + 1, 0):
                    cpy.start()
        if c < len(between):
            between[c]()
        for cpy in chunk_copies(seq, c):
            cpy.wait()
        slot = c % 2
        s = jnp.dot(qbd, kbuf[slot].astype(BF16), preferred_element_type=F32)
        kpos = c * ck + lax.broadcasted_iota(jnp.int32, (rows, ck), 1)
        dist = (past + t_row) - kpos
        s = s - slope_row * dist.astype(F32)
        m_new = jnp.maximum(m_i, jnp.max(s, axis=1, keepdims=True))
        alpha = jnp.exp(m_i - m_new)
        p = jnp.exp(s - m_new)
        l_i = alpha * l_i + jnp.sum(p, axis=1, keepdims=True)
        vslot = vbuf.at[slot]
        pv = head_dots(p.astype(BF16), lambda h: vslot[pl.ds(h, ck, stride=N_HEADS), :].astype(BF16))
        acc = alpha * acc + pv
        m_i = m_new

    outs = []
    for h in range(N_HEADS):
        r1 = slice(h * grp, h * grp + t_new)
        r2 = slice(h * grp + t_new, (h + 1) * grp)
        outs.append(_head_finish(acc[r1], l_i[r1], acc[r2], l_i[r2], lam, hn, 1.0 - lam_init))
    return jnp.concatenate(outs, axis=1)


def _inproj_prompt_body(pt_ref, x_ref, g_ref, w_ref, wkt_ref, lq1, lk1, lq2, lk2, hn_ref, qs_ref, kn_ref, vn_ref, us_ref,
                        st_ref, kt_hbm, vr_hbm,
                        pooled_ref, tail_ref, zp_ref, q_ref, kt_ref, vrow_ref, za_ref, ktb_ref, vb_ref, os_ref, pooleds_ref,
                        ext_ref, exts_ref, kbuf, vbuf, sems,
                        *, tm, tk, d_split, q_scale, nt, t_new, past, sample_kw):
    i = pl.program_id(1)
    seq = pl.program_id(0) * nt + i

    exts_ref[0:HALO, :] = st_ref[0]
    exts_ref[HALO:HALO + t_new, :] = us_ref[0]
    pos_s = past + lax.broadcasted_iota(jnp.int32, (t_new, 1), 0)
    inv_s = [1.0 / jnp.minimum(pos_s + 1, w).astype(F32) for w in POOL_WINDOWS]
    pooled_s = _window_pool(exts_ref, HALO, t_new, inv_s)
    for g in range(len(POOL_WINDOWS)):
        pooleds_ref[0, :, g * LANES:(g + 1) * LANES] = pooled_s[g]

    a, u, zp, q, proj = _project(x_ref, g_ref, w_ref, d_split, q_scale)
    zp_ref[...] = zp
    q_ref[...] = q

    def piece_values():
        v = proj(4)
        for h in range(N_HEADS):
            vrow_ref[pl.ds(h, tm, stride=N_HEADS), :] = v[:, h * LANES:(h + 1) * LANES]
        vb_ref[...] = v.astype(BF16)

    def piece_keys():
        kt = lax.dot_general(wkt_ref[...], a, (((1,), (1,)), ((), ())), preferred_element_type=F32)
        kt_ref[0] = kt
        for c in range(tm // tk):
            ktb_ref[0, c] = kt[:, c * tk:(c + 1) * tk].astype(BF16)

    def piece_gate():
        za_ref[...] = proj(5)

    def piece_pool():
        @pl.when(i == 0)
        def _():
            ext_ref[0:HALO, :] = jnp.zeros((HALO, d_split), F32)

        ext_ref[HALO:HALO + tm, :] = u
        pos = i * tm + lax.broadcasted_iota(jnp.int32, (tm, 1), 0)
        inv_cnt = [1.0 / jnp.minimum(pos + 1, w).astype(F32) for w in POOL_WINDOWS]
        pooled = _window_pool(ext_ref, HALO, tm, inv_cnt)
        for g in range(len(POOL_WINDOWS)):
            pooled_ref[:, g * LANES:(g + 1) * LANES] = pooled[g].astype(pooled_ref.dtype)
        tail = ext_ref[tm:tm + HALO, :]
        tail_ref[0] = tail
        ext_ref[0:HALO, :] = tail

    lam = _lam_value(lq1, lk1, lq2, lk2, sample_kw["lam_init"])
    os_ref[0] = _paged_attention_seq(seq, pl.num_programs(0) * nt, pt_ref, lam, hn_ref[...], qs_ref, kn_ref, vn_ref,
                                     kt_hbm, vr_hbm, kbuf, vbuf, sems, t_new=t_new, past=past,
                                     between=[piece_values, piece_keys, piece_gate, piece_pool], **sample_kw)


def _inproj_prompt(x, g, w_in, wkt, page_table, lams, hn, q_s, k_s, v_s, u_s, state_pad, cache_kt, cache_vr,
                   *, tm, tk, q_scale, cp, lam_init):
    B, S, D = x.shape
    NB, t_new, W = q_s.shape
    d_split = w_in.shape[1] // 6
    nt = S // tm
    assert NB == B * nt, "one sample sequence rides on each prompt tile step"
    n_pages = page_table.shape[1]
    page = cache_kt.shape[2]
    assert n_pages % cp == 0
    n_chunks = n_pages // cp
    assert n_chunks >= 4, "the prompt pieces are spread over four cache chunks"
    flat_spec = pl.BlockSpec((tm, d_split), lambda b, i, pt: (b * nt + i, 0))
    f32_tok = jax.ShapeDtypeStruct((B * S, d_split), F32)
    bf_tok = jax.ShapeDtypeStruct((B * S, d_split), BF16)
    const = lambda a: pl.BlockSpec(a.shape, lambda b, i, pt: (0,) * a.ndim)
    seq_spec = lambda r: pl.BlockSpec((1, r, W), lambda b, i, pt: (b * nt + i, 0, 0))
    seq_out = jax.ShapeDtypeStruct((NB, t_new, W), F32)
    body = functools.partial(
        _inproj_prompt_body, tm=tm, tk=tk, d_split=d_split, q_scale=q_scale, nt=nt, t_new=t_new, past=n_pages * page,
        sample_kw=dict(n_chunks=n_chunks, cp=cp, page=page, lam_init=lam_init))
    grid_spec = pltpu.PrefetchScalarGridSpec(
        num_scalar_prefetch=1,
        grid=(B, nt),
        in_specs=[pl.BlockSpec((tm, D), lambda b, i, pt: (b * nt + i, 0)), const(g), const(w_in), const(wkt)]
        + [const(l) for l in lams] + [const(hn), seq_spec(t_new), seq_spec(t_new), seq_spec(t_new), seq_spec(t_new),
                                      seq_spec(HALO), pl.BlockSpec(memory_space=pl.ANY),
                                      pl.BlockSpec(memory_space=pl.ANY)],
        out_specs=[flat_spec,
                   pl.BlockSpec((1, HALO, d_split), lambda b, i, pt: (b, 0, 0)),
                   flat_spec, flat_spec,
                   pl.BlockSpec((1, d_split, tm), lambda b, i, pt: (b, 0, i)),
                   pl.BlockSpec((tm * N_HEADS, LANES), lambda b, i, pt: (b * nt + i, 0)),
                   flat_spec,
                   pl.BlockSpec((1, tm // tk, d_split, tk), lambda b, i, pt: (b, i, 0, 0)),
                   flat_spec, seq_spec(t_new), seq_spec(t_new)],
        scratch_shapes=[pltpu.VMEM((HALO + tm, d_split), F32), pltpu.VMEM((HALO + t_new, W), F32),
                        pltpu.VMEM((2, W, cp * page), F32), pltpu.VMEM((2, cp * page * N_HEADS, LANES), F32),
                        pltpu.SemaphoreType.DMA((2, 2))],
    )
    return pl.pallas_call(
        body,
        grid_spec=grid_spec,
        out_shape=[bf_tok,
                   jax.ShapeDtypeStruct((B, HALO, d_split), F32),
                   f32_tok, bf_tok,
                   jax.ShapeDtypeStruct((B, d_split, S), F32),
                   jax.ShapeDtypeStruct((B * S * N_HEADS, LANES), F32),
                   f32_tok,
                   jax.ShapeDtypeStruct((B, S // tk, d_split, tk), BF16),
                   bf_tok, seq_out, seq_out],
        compiler_params=pltpu.CompilerParams(dimension_semantics=("arbitrary", "arbitrary"),
                                             vmem_limit_bytes=VMEM_LIMIT),
        name="inproj_prompt_paged_attn",
    )(page_table.reshape(-1), x.reshape(B * S, D), g, w_in, wkt, *lams, hn, q_s, k_s, v_s, u_s, state_pad,
      cache_kt, cache_vr)


def _attn_prompt_body(lq1, lk1, lq2, lk2, hn_ref, q_ref, kt_ref, v_ref, o_ref, q2_ref, m_ref, l_ref, acc_ref,
                      *, tq, lam_init):
    i = pl.program_id(1)
    rows = 2 * tq
    half = LANES // 2
    lane = lax.broadcasted_iota(jnp.int32, (tq, LANES), 1)
    for h in range(N_HEADS):
        qh = q_ref[0, :, h * LANES:(h + 1) * LANES]
        zero = jnp.zeros_like(qh)
        q2_ref[h] = jnp.concatenate([jnp.where(lane < half, qh, zero), jnp.where(lane >= half, qh, zero)], axis=0)
    m_ref[...] = jnp.full(m_ref.shape, -jnp.inf, F32)
    l_ref[...] = jnp.zeros(l_ref.shape, F32)
    acc_ref[...] = jnp.zeros(acc_ref.shape, F32)
    col = lax.broadcasted_iota(jnp.int32, (1, tq), 1).astype(F32)

    def block(j, masked):
        kpos = col + ((j - i) * tq).astype(F32)
        for h in range(N_HEADS):
            s = jnp.dot(q2_ref[h], kt_ref[0, j, h * LANES:(h + 1) * LANES, :], preferred_element_type=F32)
            s = s + (_alibi_slope(h) * LOG2E) * kpos
            if masked:
                r_i = lax.broadcasted_iota(jnp.int32, (rows, tq), 0)
                c_i = lax.broadcasted_iota(jnp.int32, (rows, tq), 1)
                s = jnp.where(c_i <= jnp.where(r_i >= tq, r_i - tq, r_i), s, -jnp.inf)
            m_old = m_ref[h]
            m_new = jnp.maximum(m_old, jnp.max(s, axis=1, keepdims=True))
            alpha = jnp.exp2(m_old - m_new)
            ps = [jnp.exp2(s[:, t * LANES:(t + 1) * LANES] - m_new) for t in range(tq // LANES)]
            l_ref[h] = alpha * l_ref[h] + functools.reduce(lambda x, y: x + y, ps)
            vb = v_ref[0, pl.ds(pl.multiple_of(j * tq, tq), tq), h * LANES:(h + 1) * LANES]
            p = jnp.concatenate(ps, axis=1).astype(BF16)
            acc_ref[h] = alpha * acc_ref[h] + jnp.dot(p, vb, preferred_element_type=F32)
            m_ref[h] = m_new

    def full_block(j, carry):
        block(j, False)
        return carry

    lax.fori_loop(0, i, full_block, 0)
    block(i, True)
    lam = _lam_value(lq1, lk1, lq2, lk2, lam_init)
    hn = hn_ref[...]
    for h in range(N_HEADS):
        l_h = jnp.sum(l_ref[h], axis=1, keepdims=True)
        o = _head_finish(acc_ref[h, :tq], l_h[:tq], acc_ref[h, tq:], l_h[tq:], lam, hn, 1.0 - lam_init)
        o_ref[0, :, h * LANES:(h + 1) * LANES] = o


def _attn_prompt(lams, hn, qb, ktb, vb, *, tq, lam_init):
    B, S, W = qb.shape
    body = functools.partial(_attn_prompt_body, tq=tq, lam_init=lam_init)
    small = lambda a: pl.BlockSpec(a.shape, lambda b, i: (0, 0))
    rows = 2 * tq
    return pl.pallas_call(
        body,
        grid=(B, S // tq),
        in_specs=[small(l) for l in lams] + [small(hn), pl.BlockSpec((1, tq, W), lambda b, i: (b, i, 0)),
                                              pl.BlockSpec((1,) + ktb.shape[1:], lambda b, i: (b, 0, 0, 0)),
                                              pl.BlockSpec((1, S, W), lambda b, i: (b, 0, 0))],
        out_specs=pl.BlockSpec((1, tq, W), lambda b, i: (b, i, 0)),
        out_shape=jax.ShapeDtypeStruct((B, S, W), F32),
        scratch_shapes=[pltpu.VMEM((N_HEADS, rows, LANES), BF16), pltpu.VMEM((N_HEADS, rows, LANES), F32),
                        pltpu.VMEM((N_HEADS, rows, LANES), F32), pltpu.VMEM((N_HEADS, rows, LANES), F32)],
        compiler_params=pltpu.CompilerParams(dimension_semantics=("arbitrary", "arbitrary"),
                                             vmem_limit_bytes=VMEM_LIMIT),
        name="attn_prompt",
    )(*lams, hn, qb, ktb, vb)


def _merge_body(h_ref, pooled_ref, zp_ref, o_ref, za_ref, p_ref, wpool_ref, spool_ref, wout_ref, wgate_ref,
                wproj_ref, nf_ref, y_ref):
    d_pool = pooled_ref.shape[-1]
    y_pool = jnp.concatenate(
        [jnp.dot(pooled_ref[:, g * LANES:(g + 1) * LANES].astype(BF16), wpool_ref[g], preferred_element_type=F32)
         for g in range(len(POOL_WINDOWS))], axis=-1) * spool_ref[...]
    ya = (y_pool * jax.nn.silu(zp_ref[...])).astype(BF16)
    yb = (o_ref[...] * jax.nn.silu(za_ref[...])).astype(BF16)
    h1 = (h_ref[...] + jnp.dot(ya, wout_ref[0:d_pool, :], preferred_element_type=F32)
          + jnp.dot(yb, wout_ref[d_pool:, :], preferred_element_type=F32))
    gate = jax.nn.sigmoid(jnp.dot(h1.astype(BF16), wgate_ref[...], preferred_element_type=F32))
    h2 = h1 + gate * jnp.dot(p_ref[...].astype(BF16), wproj_ref[...], preferred_element_type=F32)
    y_ref[...] = _rmsnorm(h2, nf_ref[...])


def _merge(h, pooled, zp, o, za, p, w_pool, s_pool, w_out, w_gate, w_proj, norm_final, *, tm):
    T, D = h.shape
    row = lambda a: pl.BlockSpec((tm, a.shape[1]), lambda i: (i, 0))
    full = lambda a: pl.BlockSpec(a.shape, lambda i: (0,) * a.ndim)
    acts = (h, pooled, zp, o, za, p)
    consts = (w_pool, s_pool, w_out, w_gate, w_proj, norm_final)
    return pl.pallas_call(
        _merge_body,
        grid=(T // tm,),
        in_specs=[row(a) for a in acts] + [full(a) for a in consts],
        out_specs=pl.BlockSpec((tm, D), lambda i: (i, 0)),
        out_shape=jax.ShapeDtypeStruct((T, D), F32),
        compiler_params=pltpu.CompilerParams(dimension_semantics=("arbitrary",), vmem_limit_bytes=VMEM_LIMIT),
        name="merge",
    )(*acts, *consts)


def kernel(x_prompt, x_sample, cache_k, cache_v, state_pool, page_table, p_prompt, p_sample, norm_in, w_in, w_pool,
           s_pool, lambda_q1, lambda_k1, lambda_q2, lambda_k2, head_norm, w_out, w_ple_gate, w_ple_proj, norm_final):
    B, S, D = x_prompt.shape
    NB, T_NEW, _ = x_sample.shape
    depth = w_in.shape[0]
    assert depth == 1, "single-layer step"
    n_phys, page = cache_k.shape[1], cache_k.shape[2]
    hp = x_prompt.reshape(B * S, D)
    hs = x_sample.reshape(NB * T_NEW, D)
    outs = {}
    for i in range(depth):
        lam_init = _lambda_init(i)
        g_in = norm_in[i].reshape(1, D)
        w_in_b = w_in[i].astype(BF16)
        lams = tuple(a[i].reshape(1, -1) for a in (lambda_q1, lambda_k1, lambda_q2, lambda_k2))
        hn = head_norm[i].reshape(1, -1)
        consts = (w_pool[i].astype(BF16), s_pool[i].reshape(1, -1), w_out[i].astype(BF16),
                  w_ple_gate[i].astype(BF16), w_ple_proj[i].astype(BF16), norm_final.reshape(1, D))
        d_split = w_in.shape[-1] // 6
        attn_scale = (d_split // N_HEADS // 2) ** -0.5

        seq = lambda a: a.reshape(NB, T_NEW, -1)
        tm_s = min(TM_SAMPLE, NB * T_NEW)
        u_s, zp_s, qb_s, k_s, v_s, za_s = _inproj_sample(hs, g_in, w_in_b, tm=tm_s, q_scale=attn_scale)
        state_pad = jnp.pad(state_pool[i], ((0, 0), (HALO - POOL_STATE_LEN, 0), (0, 0)))

        wkt = w_in[i][:, 3 * d_split:4 * d_split].T.astype(BF16)
        pooled, tail, zp, qb, kt, vrow, za, ktb, vb, o_s, pooled_s = _inproj_prompt(
            hp.reshape(B, S, D), g_in, w_in_b, wkt, page_table, lams, hn, seq(qb_s), seq(k_s), seq(v_s), seq(u_s),
            state_pad, jnp.transpose(cache_k[i], (0, 2, 3, 4, 1)).reshape(n_phys, -1, page),
            cache_v[i].reshape(n_phys, page * N_HEADS, -1),
            tm=TM_PROMPT, tk=TQ_PROMPT, q_scale=attn_scale * LOG2E, cp=PAGES_PER_CHUNK, lam_init=lam_init)
        o = _attn_prompt(lams, hn, qb.reshape(B, S, -1), ktb, vb.reshape(B, S, -1), tq=TQ_PROMPT, lam_init=lam_init)
        y_prompt = _merge(hp, pooled, zp, o.reshape(B * S, -1), za, p_prompt[i].reshape(B * S, -1), *consts,
                          tm=TM_PROMPT)
        outs["k_p"] = jnp.transpose(kt.reshape(B, N_HEADS, 2, -1, S), (0, 4, 1, 2, 3))
        outs["v_p"] = vrow.reshape(B, S, N_HEADS, -1)
        outs["pool_p"] = tail[:, HALO - POOL_STATE_LEN:]

        y_sample = _merge(hs, pooled_s.reshape(NB * T_NEW, -1), zp_s, o_s.reshape(NB * T_NEW, -1), za_s,
                          p_sample[i].reshape(NB * T_NEW, -1), *consts, tm=tm_s)
        outs["k_s"] = k_s.reshape(NB, T_NEW, N_HEADS, 2, -1)
        outs["v_s"] = v_s.reshape(NB, T_NEW, N_HEADS, -1)
        outs["pool_s"] = jnp.concatenate([state_pool[i], seq(u_s)], axis=1)[:, -POOL_STATE_LEN:]
    return (y_prompt.reshape(B, S, D), y_sample.reshape(NB, T_NEW, D),
            outs["k_p"][None], outs["v_p"][None], outs["pool_p"][None],
            outs["k_s"][None], outs["v_s"][None], outs["pool_s"][None])
```

```python
import functools
import math

import jax
import jax.numpy as jnp
from jax import lax
from jax.experimental import pallas as pl
from jax.experimental.pallas import tpu as pltpu

F32 = jnp.float32
BF16 = jnp.bfloat16

EPS = 1e-6
POOL_WINDOWS = (2, 4, 8, 16)
POOL_STATE_LEN = max(POOL_WINDOWS) - 1
HALO = 16
N_HEADS = 4
LANES = 128
VMEM_LIMIT = 56 * 1024 * 1024
LOG2E = math.log2(math.e)
TM_PROMPT = 1024
TQ_PROMPT = 256
TK_PROMPT = 512
TM_SAMPLE = 256
PAGES_PER_CHUNK = 16


def _lambda_init(layer):
    return 0.8 - 0.6 * math.exp(-0.3 * layer)


def _alibi_slope(h):
    return 2.0 ** (-8.0 * (h + 1) / N_HEADS)


def _rmsnorm(xf, g):
    return xf * lax.rsqrt(jnp.mean(xf * xf, axis=-1, keepdims=True) + EPS) * g


def _lam_value(lq1, lk1, lq2, lk2, lam_init):
    a = jnp.sum(lq1[...] * lk1[...], axis=-1, keepdims=True)
    b = jnp.sum(lq2[...] * lk2[...], axis=-1, keepdims=True)
    return jnp.exp(a) - jnp.exp(b) + lam_init


def _window_pool(ext_ref, base, rows, inv_cnt):
    outs = []
    for g, w in enumerate(POOL_WINDOWS):
        cols = slice(g * LANES, (g + 1) * LANES)
        cur = ext_ref[base:base + rows, cols]
        acc = cur
        for j in range(1, w):
            acc = acc + ext_ref[base - j:base - j + rows, cols]
        outs.append(acc * inv_cnt[g] - cur)
    return outs


def _head_finish(o1, l1, o2, l2, lam, hn, out_scale):
    o = o1 / l1 - lam * (o2 / l2)
    return _rmsnorm(o, hn) * out_scale


def _project(x_ref, g_ref, w_ref, d_split, q_scale):
    a = _rmsnorm(x_ref[...], g_ref[...]).astype(BF16)

    def proj(c):
        return jnp.dot(a, w_ref[:, c * d_split:(c + 1) * d_split], preferred_element_type=F32)

    return a, proj(0), proj(1), (proj(2) * q_scale).astype(BF16), proj


def _inproj_sample_body(x_ref, g_ref, w_ref, u_ref, zp_ref, q_ref, k_ref, v_ref, za_ref, *, d_split, q_scale):
    _, u, zp, q, proj = _project(x_ref, g_ref, w_ref, d_split, q_scale)
    u_ref[...] = u
    zp_ref[...] = zp
    q_ref[...] = q
    k_ref[...] = proj(3)
    v_ref[...] = proj(4)
    za_ref[...] = proj(5)


def _inproj_sample(x2d, g, w_in, *, tm, q_scale):
    T, D = x2d.shape
    d_split = w_in.shape[1] // 6
    spec = pl.BlockSpec((tm, d_split), lambda i: (i, 0))
    f32_tok = jax.ShapeDtypeStruct((T, d_split), F32)
    body = functools.partial(_inproj_sample_body, d_split=d_split, q_scale=q_scale)
    return pl.pallas_call(
        body,
        grid=(T // tm,),
        in_specs=[pl.BlockSpec((tm, D), lambda i: (i, 0)),
                  pl.BlockSpec((1, D), lambda i: (0, 0)),
                  pl.BlockSpec(w_in.shape, lambda i: (0, 0))],
        out_specs=[spec] * 6,
        out_shape=[f32_tok, f32_tok, jax.ShapeDtypeStruct((T, d_split), BF16), f32_tok, f32_tok, f32_tok],
        compiler_params=pltpu.CompilerParams(dimension_semantics=("arbitrary",), vmem_limit_bytes=VMEM_LIMIT),
        name="inproj_sample",
    )(x2d, g, w_in)


def _inproj_prompt_body(x_ref, g_ref, w_ref, wkt_ref, us_ref, st_ref,
                        pooled_ref, tail_ref, zp_ref, q_ref, kt_ref, vrow_ref, za_ref, ktb_ref, vb_ref, pooleds_ref,
                        ext_ref, exts_ref, *, tm, tk, d_split, q_scale, nt, t_new, past):
    i = pl.program_id(1)
    pos_s = past + lax.broadcasted_iota(jnp.int32, (t_new, 1), 0)
    inv_s = [1.0 / jnp.minimum(pos_s + 1, w).astype(F32) for w in POOL_WINDOWS]
    for sq in range(st_ref.shape[0]):
        exts_ref[0:HALO, :] = st_ref[sq]
        exts_ref[HALO:HALO + t_new, :] = us_ref[sq]
        pooled_s = _window_pool(exts_ref, HALO, t_new, inv_s)
        for g in range(len(POOL_WINDOWS)):
            pooleds_ref[sq, :, g * LANES:(g + 1) * LANES] = pooled_s[g]

    a, u, zp, q, proj = _project(x_ref, g_ref, w_ref, d_split, q_scale)
    zp_ref[...] = zp
    q_ref[...] = q
    v = proj(4)
    for h in range(N_HEADS):
        vrow_ref[pl.ds(h, tm, stride=N_HEADS), :] = v[:, h * LANES:(h + 1) * LANES]
    vb_ref[...] = v.astype(BF16)
    kt = lax.dot_general(wkt_ref[...], a, (((1,), (1,)), ((), ())), preferred_element_type=F32)
    kt_ref[0] = kt
    for c in range(tm // tk):
        ktb_ref[0, c] = kt[:, c * tk:(c + 1) * tk].astype(BF16)
    za_ref[...] = proj(5)

    @pl.when(i == 0)
    def _():
        ext_ref[0:HALO, :] = jnp.zeros((HALO, d_split), F32)

    ext_ref[HALO:HALO + tm, :] = u
    pos = i * tm + lax.broadcasted_iota(jnp.int32, (tm, 1), 0)
    inv_cnt = [1.0 / jnp.minimum(pos + 1, w).astype(F32) for w in POOL_WINDOWS]
    pooled = _window_pool(ext_ref, HALO, tm, inv_cnt)
    for g in range(len(POOL_WINDOWS)):
        pooled_ref[:, g * LANES:(g + 1) * LANES] = pooled[g].astype(pooled_ref.dtype)
    tail = ext_ref[tm:tm + HALO, :]
    tail_ref[0] = tail
    ext_ref[0:HALO, :] = tail


def _inproj_prompt(x, g, w_in, wkt, u_s, state_pad, *, tm, tk, q_scale, past):
    B, S, D = x.shape
    NB, t_new, W = u_s.shape
    d_split = w_in.shape[1] // 6
    nt = S // tm
    assert NB % (B * nt) == 0, "the sample sequences' pooling windows ride on the prompt tile steps"
    spt = NB // (B * nt)
    flat_spec = pl.BlockSpec((tm, d_split), lambda b, i: (b * nt + i, 0))
    f32_tok = jax.ShapeDtypeStruct((B * S, d_split), F32)
    bf_tok = jax.ShapeDtypeStruct((B * S, d_split), BF16)
    const = lambda a: pl.BlockSpec(a.shape, lambda b, i: (0,) * a.ndim)
    seq_spec = lambda r: pl.BlockSpec((spt, r, W), lambda b, i: (b * nt + i, 0, 0))
    body = functools.partial(_inproj_prompt_body, tm=tm, tk=tk, d_split=d_split, q_scale=q_scale, nt=nt, t_new=t_new,
                             past=past)
    return pl.pallas_call(
        body,
        grid=(B, nt),
        in_specs=[pl.BlockSpec((tm, D), lambda b, i: (b * nt + i, 0)), const(g), const(w_in), const(wkt),
                  seq_spec(t_new), seq_spec(HALO)],
        out_specs=[flat_spec,
                   pl.BlockSpec((1, HALO, d_split), lambda b, i: (b, 0, 0)),
                   flat_spec, flat_spec,
                   pl.BlockSpec((1, d_split, tm), lambda b, i: (b, 0, i)),
                   pl.BlockSpec((tm * N_HEADS, LANES), lambda b, i: (b * nt + i, 0)),
                   flat_spec,
                   pl.BlockSpec((1, tm // tk, d_split, tk), lambda b, i: (b, i, 0, 0)),
                   flat_spec, seq_spec(t_new)],
        out_shape=[bf_tok,
                   jax.ShapeDtypeStruct((B, HALO, d_split), F32),
                   f32_tok, bf_tok,
                   jax.ShapeDtypeStruct((B, d_split, S), F32),
                   jax.ShapeDtypeStruct((B * S * N_HEADS, LANES), F32),
                   f32_tok,
                   jax.ShapeDtypeStruct((B, S // tk, d_split, tk), BF16),
                   bf_tok, jax.ShapeDtypeStruct((NB, t_new, W), F32)],
        scratch_shapes=[pltpu.VMEM((HALO + tm, d_split), F32), pltpu.VMEM((HALO + t_new, W), F32)],
        compiler_params=pltpu.CompilerParams(dimension_semantics=("arbitrary", "arbitrary"),
                                             vmem_limit_bytes=VMEM_LIMIT),
        name="inproj_prompt",
    )(x.reshape(B * S, D), g, w_in, wkt, u_s, state_pad)


class _PagedAttention:
    def __init__(self, step, n_steps, pt_ref, qs_ref, kn_ref, vn_ref, kt_hbm, vr_hbm, os_ref, kbuf, vbuf, sems, qbd_ref,
                 m_ref, l_ref, acc_ref, *, n_chunks, cp, page, t_new, past, lam_init):
        self.__dict__.update(locals())
        self.ck = cp * page
        self.grp = 2 * t_new
        self.rows = N_HEADS * self.grp
        self.steps_per_seq = n_chunks // 2
        self.part = step % self.steps_per_seq
        r_i = lax.broadcasted_iota(jnp.int32, (self.rows, 1), 0)
        self.t_row = r_i % t_new
        self.slope_row = jnp.exp2(-8.0 * (r_i // self.grp + 1).astype(F32) / N_HEADS)

    def _copies(self, n, slot):
        out = []
        for p in range(self.cp):
            pg = self.pt_ref[n * self.cp + p]
            out.append(pltpu.make_async_copy(self.kt_hbm.at[pg], self.kbuf.at[slot, :, pl.ds(p * self.page, self.page)],
                                             self.sems.at[0, slot]))
            out.append(pltpu.make_async_copy(
                self.vr_hbm.at[pg], self.vbuf.at[slot, pl.ds(p * self.page * N_HEADS, self.page * N_HEADS), :],
                self.sems.at[1, slot]))
        return out

    def _head_dots(self, p, v_of_head):
        g = self.grp
        return jnp.concatenate(
            [jnp.dot(p[h * g:(h + 1) * g], v_of_head(h), preferred_element_type=F32) for h in range(N_HEADS)], axis=0)

    def prime(self):
        @pl.when(self.step == 0)
        def _():
            for t in range(2):
                for cpy in self._copies(t, t):
                    cpy.start()

    def begin_sequence(self):
        @pl.when(self.part == 0)
        def _():
            rows, t_new, W = self.rows, self.t_new, self.qs_ref.shape[-1]
            qf = self.qs_ref[0].astype(F32)
            qt = jnp.concatenate([qf] * (2 * N_HEADS), axis=0)
            col_grp = lax.broadcasted_iota(jnp.int32, (rows, W), 1) // (W // (2 * N_HEADS))
            row_grp = lax.broadcasted_iota(jnp.int32, (rows, W), 0) // t_new
            qbd = jnp.where(col_grp == row_grp, qt, 0.0).astype(BF16)
            self.qbd_ref[...] = qbd
            s = lax.dot_general(qbd, self.kn_ref[0].astype(BF16), (((1,), (1,)), ((), ())),
                                preferred_element_type=F32)
            dist = self.t_row - lax.broadcasted_iota(jnp.int32, (rows, t_new), 1)
            s = jnp.where(dist >= 0, s - self.slope_row * dist.astype(F32), -jnp.inf)
            m0 = jnp.max(s, axis=1, keepdims=True)
            p = jnp.exp(s - m0)
            lane = lax.broadcasted_iota(jnp.int32, (rows, LANES), 1)
            self.m_ref[...] = jnp.broadcast_to(m0, (rows, LANES))
            self.l_ref[...] = jnp.where(lane == 0, jnp.sum(p, axis=1, keepdims=True), 0.0)
            vn = self.vn_ref[0].astype(BF16)
            self.acc_ref[...] = self._head_dots(p.astype(BF16), lambda h: vn[:, h * LANES:(h + 1) * LANES])

    def chunk(self, t):
        n = self.step * 2 + t
        for cpy in self._copies(n, t):
            cpy.wait()
        rows, ck = self.rows, self.ck
        s = jnp.dot(self.qbd_ref[...], self.kbuf[t].astype(BF16), preferred_element_type=F32)
        kpos = (self.part * 2 + t) * ck + lax.broadcasted_iota(jnp.int32, (rows, ck), 1)
        dist = (self.past + self.t_row) - kpos
        s = s - self.slope_row * dist.astype(F32)
        m_old = self.m_ref[...]
        m_new = jnp.maximum(m_old, jnp.max(s, axis=1, keepdims=True))
        alpha = jnp.exp(m_old - m_new)
        ps = [jnp.exp(s[:, u * LANES:(u + 1) * LANES] - m_new) for u in range(ck // LANES)]
        self.l_ref[...] = alpha * self.l_ref[...] + functools.reduce(lambda x, y: x + y, ps)
        vslot = self.vbuf.at[t]
        p = jnp.concatenate(ps, axis=1).astype(BF16)
        pv = self._head_dots(p, lambda h: vslot[pl.ds(h, ck, stride=N_HEADS), :].astype(BF16))
        self.acc_ref[...] = alpha * self.acc_ref[...] + pv
        self.m_ref[...] = m_new

        @pl.when(n + 2 < self.n_steps * 2)
        def _():
            for cpy in self._copies(n + 2, t):
                cpy.start()

    def end_sequence(self, lam, hn):
        @pl.when(self.part == self.steps_per_seq - 1)
        def _():
            g, t_new = self.grp, self.t_new
            acc = self.acc_ref[...]
            l_all = jnp.sum(self.l_ref[...], axis=1, keepdims=True)
            for h in range(N_HEADS):
                r1 = slice(h * g, h * g + t_new)
                r2 = slice(h * g + t_new, (h + 1) * g)
                o = _head_finish(acc[r1], l_all[r1], acc[r2], l_all[r2], lam, hn, 1.0 - self.lam_init)
                self.os_ref[0, :, h * LANES:(h + 1) * LANES] = o


def _attn_prompt_body(pt_ref, lq1, lk1, lq2, lk2, hn_ref, q_ref, kt_ref, v_ref, qs_ref, kn_ref, vn_ref, kt_hbm, vr_hbm,
                      h_ref, pooled_ref, zp_ref, za_ref, p_ref, wpool_ref, spool_ref, wout_ref, wgate_ref, wproj_ref,
                      nf_ref,
                      y_ref, os_ref, q2_ref, m_ref, l_ref, acc_ref, kbuf, vbuf, sems, qbd_ref, ms_ref, ls_ref, accs_ref,
                      *, tq, tk, lam_init, sample_kw):
    i = pl.program_id(1)
    nq = pl.num_programs(1)
    step = pl.program_id(0) * nq + i
    paged = _PagedAttention(step, pl.num_programs(0) * nq, pt_ref, qs_ref, kn_ref, vn_ref, kt_hbm, vr_hbm, os_ref,
                            kbuf, vbuf, sems, qbd_ref, ms_ref, ls_ref, accs_ref, lam_init=lam_init, **sample_kw)
    paged.prime()
    paged.begin_sequence()
    paged.chunk(0)

    rows = 2 * tq
    half = LANES // 2
    lane = lax.broadcasted_iota(jnp.int32, (tq, LANES), 1)
    for h in range(N_HEADS):
        qh = q_ref[0, :, h * LANES:(h + 1) * LANES]
        zero = jnp.zeros_like(qh)
        q2_ref[h] = jnp.concatenate([jnp.where(lane < half, qh, zero), jnp.where(lane >= half, qh, zero)], axis=0)
    m_ref[...] = jnp.full(m_ref.shape, -jnp.inf, F32)
    l_ref[...] = jnp.zeros(l_ref.shape, F32)
    acc_ref[...] = jnp.zeros(acc_ref.shape, F32)
    col = lax.broadcasted_iota(jnp.int32, (1, tk), 1).astype(F32)
    n_full = (i * tq) // tk

    def block(j, masked):
        first = j * tk - i * tq
        kpos = col + first.astype(F32)
        for h in range(N_HEADS):
            s = jnp.dot(q2_ref[h], kt_ref[0, j, h * LANES:(h + 1) * LANES, :], preferred_element_type=F32)
            s = s + (_alibi_slope(h) * LOG2E) * kpos
            if masked:
                r_i = lax.broadcasted_iota(jnp.int32, (rows, tk), 0)
                c_i = lax.broadcasted_iota(jnp.int32, (rows, tk), 1)
                s = jnp.where(c_i + first <= jnp.where(r_i >= tq, r_i - tq, r_i), s, -jnp.inf)
            m_old = m_ref[h]
            m_new = jnp.maximum(m_old, jnp.max(s, axis=1, keepdims=True))
            alpha = jnp.exp2(m_old - m_new)
            ps = [jnp.exp2(s[:, t * LANES:(t + 1) * LANES] - m_new) for t in range(tk // LANES)]
            l_ref[h] = alpha * l_ref[h] + functools.reduce(lambda x, y: x + y, ps)
            vb = v_ref[0, pl.ds(pl.multiple_of(j * tk, tk), tk), h * LANES:(h + 1) * LANES]
            p = jnp.concatenate(ps, axis=1).astype(BF16)
            acc_ref[h] = alpha * acc_ref[h] + jnp.dot(p, vb, preferred_element_type=F32)
            m_ref[h] = m_new

    def full_block(j, carry):
        block(j, False)
        return carry

    lax.fori_loop(0, n_full, full_block, 0)
    paged.chunk(1)
    block(n_full, True)
    lam = _lam_value(lq1, lk1, lq2, lk2, lam_init)
    hn = hn_ref[...]
    heads = []
    for h in range(N_HEADS):
        l_h = jnp.sum(l_ref[h], axis=1, keepdims=True)
        heads.append(_head_finish(acc_ref[h, :tq], l_h[:tq], acc_ref[h, tq:], l_h[tq:], lam, hn, 1.0 - lam_init))
    y_ref[...] = _merge_math(jnp.concatenate(heads, axis=1), h_ref, pooled_ref, zp_ref, za_ref, p_ref, wpool_ref,
                             spool_ref, wout_ref, wgate_ref, wproj_ref, nf_ref)
    paged.end_sequence(lam, hn)


def _attn_prompt(page_table, lams, hn, qb, ktb, vb, q_s, k_s, v_s, cache_kt, cache_vr, merge_acts, merge_consts,
                 *, tq, cp, lam_init):
    B, S, W = qb.shape
    D = merge_acts[0].shape[1]
    NB, t_new, _ = q_s.shape
    tk = ktb.shape[-1]
    assert tk % tq == 0
    nq = S // tq
    n_pages = page_table.shape[1]
    page = cache_kt.shape[2]
    n_chunks = n_pages // cp
    assert n_pages % cp == 0 and n_chunks % 2 == 0
    steps_per_seq = n_chunks // 2
    assert NB * steps_per_seq == B * nq, "two cache chunks of one sample sequence ride on each prompt tile step"
    srows = 2 * N_HEADS * t_new
    body = functools.partial(_attn_prompt_body, tq=tq, tk=tk, lam_init=lam_init,
                             sample_kw=dict(n_chunks=n_chunks, cp=cp, page=page, t_new=t_new, past=n_pages * page))
    small = lambda a: pl.BlockSpec(a.shape, lambda b, i, pt: (0,) * a.ndim)
    seq_spec = pl.BlockSpec((1, t_new, W), lambda b, i, pt: ((b * nq + i) // steps_per_seq, 0, 0))
    tok = lambda a: pl.BlockSpec((tq, a.shape[1]), lambda b, i, pt: (b * nq + i, 0))
    rows = 2 * tq
    grid_spec = pltpu.PrefetchScalarGridSpec(
        num_scalar_prefetch=1,
        grid=(B, nq),
        in_specs=[small(l) for l in lams] + [small(hn), pl.BlockSpec((1, tq, W), lambda b, i, pt: (b, i, 0)),
                                              pl.BlockSpec((1,) + ktb.shape[1:], lambda b, i, pt: (b, 0, 0, 0)),
                                              pl.BlockSpec((1, S, W), lambda b, i, pt: (b, 0, 0)),
                                              seq_spec, seq_spec, seq_spec,
                                              pl.BlockSpec(memory_space=pl.ANY), pl.BlockSpec(memory_space=pl.ANY)]
        + [tok(a) for a in merge_acts] + [small(a) for a in merge_consts],
        out_specs=[pl.BlockSpec((tq, D), lambda b, i, pt: (b * nq + i, 0)), seq_spec],
        scratch_shapes=[pltpu.VMEM((N_HEADS, rows, LANES), BF16), pltpu.VMEM((N_HEADS, rows, LANES), F32),
                        pltpu.VMEM((N_HEADS, rows, LANES), F32), pltpu.VMEM((N_HEADS, rows, LANES), F32),
                        pltpu.VMEM((2, W, cp * page), F32), pltpu.VMEM((2, cp * page * N_HEADS, LANES), F32),
                        pltpu.SemaphoreType.DMA((2, 2)),
                        pltpu.VMEM((srows, W), BF16), pltpu.VMEM((srows, LANES), F32), pltpu.VMEM((srows, LANES), F32),
                        pltpu.VMEM((srows, LANES), F32)],
    )
    return pl.pallas_call(
        body,
        grid_spec=grid_spec,
        out_shape=[jax.ShapeDtypeStruct((B * S, D), F32), jax.ShapeDtypeStruct((NB, t_new, W), F32)],
        compiler_params=pltpu.CompilerParams(dimension_semantics=("arbitrary", "arbitrary"),
                                             vmem_limit_bytes=VMEM_LIMIT),
        name="attn_merge_prompt_paged_attn",
    )(page_table.reshape(-1), *lams, hn, qb, ktb, vb, q_s, k_s, v_s, cache_kt, cache_vr, *merge_acts, *merge_consts)


def _merge_math(o, h_ref, pooled_ref, zp_ref, za_ref, p_ref, wpool_ref, spool_ref, wout_ref, wgate_ref, wproj_ref,
                nf_ref):
    d_pool = pooled_ref.shape[-1]
    y_pool = jnp.concatenate(
        [jnp.dot(pooled_ref[:, g * LANES:(g + 1) * LANES].astype(BF16), wpool_ref[g], preferred_element_type=F32)
         for g in range(len(POOL_WINDOWS))], axis=-1) * spool_ref[...]
    ya = (y_pool * jax.nn.silu(zp_ref[...])).astype(BF16)
    yb = (o * jax.nn.silu(za_ref[...])).astype(BF16)
    h1 = (h_ref[...] + jnp.dot(ya, wout_ref[0:d_pool, :], preferred_element_type=F32)
          + jnp.dot(yb, wout_ref[d_pool:, :], preferred_element_type=F32))
    gate = jax.nn.sigmoid(jnp.dot(h1.astype(BF16), wgate_ref[...], preferred_element_type=F32))
    h2 = h1 + gate * jnp.dot(p_ref[...].astype(BF16), wproj_ref[...], preferred_element_type=F32)
    return _rmsnorm(h2, nf_ref[...])


def _merge_body(h_ref, pooled_ref, zp_ref, o_ref, za_ref, p_ref, wpool_ref, spool_ref, wout_ref, wgate_ref,
                wproj_ref, nf_ref, y_ref):
    y_ref[...] = _merge_math(o_ref[...], h_ref, pooled_ref, zp_ref, za_ref, p_ref, wpool_ref, spool_ref, wout_ref,
                             wgate_ref, wproj_ref, nf_ref)


def _merge(h, pooled, zp, o, za, p, w_pool, s_pool, w_out, w_gate, w_proj, norm_final, *, tm):
    T, D = h.shape
    row = lambda a: pl.BlockSpec((tm, a.shape[1]), lambda i: (i, 0))
    full = lambda a: pl.BlockSpec(a.shape, lambda i: (0,) * a.ndim)
    acts = (h, pooled, zp, o, za, p)
    consts = (w_pool, s_pool, w_out, w_gate, w_proj, norm_final)
    return pl.pallas_call(
        _merge_body,
        grid=(T // tm,),
        in_specs=[row(a) for a in acts] + [full(a) for a in consts],
        out_specs=pl.BlockSpec((tm, D), lambda i: (i, 0)),
        out_shape=jax.ShapeDtypeStruct((T, D), F32),
        compiler_params=pltpu.CompilerParams(dimension_semantics=("arbitrary",), vmem_limit_bytes=VMEM_LIMIT),
        name="merge",
    )(*acts, *consts)


def kernel(x_prompt, x_sample, cache_k, cache_v, state_pool, page_table, p_prompt, p_sample, norm_in, w_in, w_pool,
           s_pool, lambda_q1, lambda_k1, lambda_q2, lambda_k2, head_norm, w_out, w_ple_gate, w_ple_proj, norm_final):
    B, S, D = x_prompt.shape
    NB, T_NEW, _ = x_sample.shape
    depth = w_in.shape[0]
    assert depth == 1, "single-layer step"
    n_phys, page = cache_k.shape[1], cache_k.shape[2]
    hp = x_prompt.reshape(B * S, D)
    hs = x_sample.reshape(NB * T_NEW, D)
    outs = {}
    for i in range(depth):
        lam_init = _lambda_init(i)
        g_in = norm_in[i].reshape(1, D)
        w_in_b = w_in[i].astype(BF16)
        lams = tuple(a[i].reshape(1, -1) for a in (lambda_q1, lambda_k1, lambda_q2, lambda_k2))
        hn = head_norm[i].reshape(1, -1)
        consts = (w_pool[i].astype(BF16), s_pool[i].reshape(1, -1), w_out[i].astype(BF16),
                  w_ple_gate[i].astype(BF16), w_ple_proj[i].astype(BF16), norm_final.reshape(1, D))
        d_split = w_in.shape[-1] // 6
        attn_scale = (d_split // N_HEADS // 2) ** -0.5

        seq = lambda a: a.reshape(NB, T_NEW, -1)
        tm_s = min(TM_SAMPLE, NB * T_NEW)
        u_s, zp_s, qb_s, k_s, v_s, za_s = _inproj_sample(hs, g_in, w_in_b, tm=tm_s, q_scale=attn_scale)
        state_pad = jnp.pad(state_pool[i], ((0, 0), (HALO - POOL_STATE_LEN, 0), (0, 0)))

        wkt = w_in[i][:, 3 * d_split:4 * d_split].T.astype(BF16)
        pooled, tail, zp, qb, kt, vrow, za, ktb, vb, pooled_s = _inproj_prompt(
            hp.reshape(B, S, D), g_in, w_in_b, wkt, seq(u_s), state_pad,
            tm=TM_PROMPT, tk=TK_PROMPT, q_scale=attn_scale * LOG2E, past=page_table.shape[1] * page)
        y_prompt, o_s = _attn_prompt(page_table, lams, hn, qb.reshape(B, S, -1), ktb, vb.reshape(B, S, -1),
                                     seq(qb_s), seq(k_s), seq(v_s),
                                     jnp.transpose(cache_k[i], (0, 2, 3, 4, 1)).reshape(n_phys, -1, page),
                                     cache_v[i].reshape(n_phys, page * N_HEADS, -1),
                                     (hp, pooled, zp, za, p_prompt[i].reshape(B * S, -1)), consts,
                                     tq=TQ_PROMPT, cp=PAGES_PER_CHUNK, lam_init=lam_init)
        outs["k_p"] = jnp.transpose(kt.reshape(B, N_HEADS, 2, -1, S), (0, 4, 1, 2, 3))
        outs["v_p"] = vrow.reshape(B, S, N_HEADS, -1)
        outs["pool_p"] = tail[:, HALO - POOL_STATE_LEN:]

        y_sample = _merge(hs, pooled_s.reshape(NB * T_NEW, -1), zp_s, o_s.reshape(NB * T_NEW, -1), za_s,
                          p_sample[i].reshape(NB * T_NEW, -1), *consts, tm=tm_s)
        outs["k_s"] = k_s.reshape(NB, T_NEW, N_HEADS, 2, -1)
        outs["v_s"] = v_s.reshape(NB, T_NEW, N_HEADS, -1)
        outs["pool_s"] = jnp.concatenate([state_pool[i], seq(u_s)], axis=1)[:, -POOL_STATE_LEN:]
    return (y_prompt.reshape(B, S, D), y_sample.reshape(NB, T_NEW, D),
            outs["k_p"][None], outs["v_p"][None], outs["pool_p"][None],
            outs["k_s"][None], outs["v_s"][None], outs["pool_s"][None])
```

```python
import functools
import math

import jax
import jax.numpy as jnp
from jax import lax
from jax.experimental import pallas as pl
from jax.experimental.pallas import tpu as pltpu

F32 = jnp.float32
BF16 = jnp.bfloat16

EPS = 1e-6
POOL_WINDOWS = (2, 4, 8, 16)
POOL_STATE_LEN = max(POOL_WINDOWS) - 1
HALO = 16
N_HEADS = 4
LANES = 128
VMEM_LIMIT = 56 * 1024 * 1024
LOG2E = math.log2(math.e)
TM_PROMPT = 1024
TQ_PROMPT = 256
TK_PROMPT = 512
TM_SAMPLE = 256
PAGES_PER_CHUNK = 16


def _lambda_init(layer):
    return 0.8 - 0.6 * math.exp(-0.3 * layer)


def _alibi_slope(h):
    return 2.0 ** (-8.0 * (h + 1) / N_HEADS)


def _rmsnorm(xf, g):
    return xf * lax.rsqrt(jnp.mean(xf * xf, axis=-1, keepdims=True) + EPS) * g


def _lam_value(lq1, lk1, lq2, lk2, lam_init):
    a = jnp.sum(lq1[...] * lk1[...], axis=-1, keepdims=True)
    b = jnp.sum(lq2[...] * lk2[...], axis=-1, keepdims=True)
    return jnp.exp(a) - jnp.exp(b) + lam_init


def _window_pool(ext_ref, base, rows, inv_cnt):
    outs = []
    for g, w in enumerate(POOL_WINDOWS):
        cols = slice(g * LANES, (g + 1) * LANES)
        cur = ext_ref[base:base + rows, cols]
        acc = cur
        for j in range(1, w):
            acc = acc + ext_ref[base - j:base - j + rows, cols]
        outs.append(acc * inv_cnt[g] - cur)
    return outs


def _head_finish(o1, l1, o2, l2, lam, hn, out_scale):
    o = o1 / l1 - lam * (o2 / l2)
    return _rmsnorm(o, hn) * out_scale


def _project(x_ref, g_ref, w_ref, d_split, q_scale):
    a = _rmsnorm(x_ref[...], g_ref[...]).astype(BF16)

    def proj(c):
        return jnp.dot(a, w_ref[:, c * d_split:(c + 1) * d_split], preferred_element_type=F32)

    return a, proj(0), proj(1), (proj(2) * q_scale).astype(BF16), proj


def _inproj_sample_body(x_ref, g_ref, w_ref, u_ref, zp_ref, q_ref, k_ref, v_ref, za_ref, *, d_split, q_scale):
    _, u, zp, q, proj = _project(x_ref, g_ref, w_ref, d_split, q_scale)
    u_ref[...] = u
    zp_ref[...] = zp
    q_ref[...] = q
    k_ref[...] = proj(3)
    v_ref[...] = proj(4)
    za_ref[...] = proj(5)


def _inproj_sample(x2d, g, w_in, *, tm, q_scale):
    T, D = x2d.shape
    d_split = w_in.shape[1] // 6
    spec = pl.BlockSpec((tm, d_split), lambda i: (i, 0))
    f32_tok = jax.ShapeDtypeStruct((T, d_split), F32)
    body = functools.partial(_inproj_sample_body, d_split=d_split, q_scale=q_scale)
    return pl.pallas_call(
        body,
        grid=(T // tm,),
        in_specs=[pl.BlockSpec((tm, D), lambda i: (i, 0)),
                  pl.BlockSpec((1, D), lambda i: (0, 0)),
                  pl.BlockSpec(w_in.shape, lambda i: (0, 0))],
        out_specs=[spec] * 6,
        out_shape=[f32_tok, f32_tok, jax.ShapeDtypeStruct((T, d_split), BF16), f32_tok, f32_tok, f32_tok],
        compiler_params=pltpu.CompilerParams(dimension_semantics=("arbitrary",), vmem_limit_bytes=VMEM_LIMIT),
        name="inproj_sample",
    )(x2d, g, w_in)


def _inproj_prompt_body(x_ref, g_ref, w_ref, wkt_ref, us_ref, st_ref,
                        pooled_ref, tail_ref, zp_ref, q_ref, kt_ref, vrow_ref, za_ref, ktb_ref, vb_ref, pooleds_ref,
                        ext_ref, exts_ref, *, tm, tk, d_split, q_scale, nt, t_new, past):
    i = pl.program_id(1)
    pos_s = past + lax.broadcasted_iota(jnp.int32, (t_new, 1), 0)
    inv_s = [1.0 / jnp.minimum(pos_s + 1, w).astype(F32) for w in POOL_WINDOWS]
    for sq in range(st_ref.shape[0]):
        exts_ref[0:HALO, :] = st_ref[sq]
        exts_ref[HALO:HALO + t_new, :] = us_ref[sq]
        pooled_s = _window_pool(exts_ref, HALO, t_new, inv_s)
        for g in range(len(POOL_WINDOWS)):
            pooleds_ref[sq, :, g * LANES:(g + 1) * LANES] = pooled_s[g]

    a, u, zp, q, proj = _project(x_ref, g_ref, w_ref, d_split, q_scale)
    zp_ref[...] = zp
    q_ref[...] = q
    v = proj(4)
    for h in range(N_HEADS):
        vrow_ref[pl.ds(h, tm, stride=N_HEADS), :] = v[:, h * LANES:(h + 1) * LANES]
    vb_ref[...] = v.astype(BF16)
    kt = lax.dot_general(wkt_ref[...], a, (((1,), (1,)), ((), ())), preferred_element_type=F32)
    kt_ref[0] = kt
    for c in range(tm // tk):
        ktb_ref[0, c] = kt[:, c * tk:(c + 1) * tk].astype(BF16)
    za_ref[...] = proj(5)

    @pl.when(i == 0)
    def _():
        ext_ref[0:HALO, :] = jnp.zeros((HALO, d_split), F32)

    ext_ref[HALO:HALO + tm, :] = u
    pos = i * tm + lax.broadcasted_iota(jnp.int32, (tm, 1), 0)
    inv_cnt = [1.0 / jnp.minimum(pos + 1, w).astype(F32) for w in POOL_WINDOWS]
    pooled = _window_pool(ext_ref, HALO, tm, inv_cnt)
    for g in range(len(POOL_WINDOWS)):
        pooled_ref[:, g * LANES:(g + 1) * LANES] = pooled[g].astype(pooled_ref.dtype)
    tail = ext_ref[tm:tm + HALO, :]
    tail_ref[0] = tail
    ext_ref[0:HALO, :] = tail


def _inproj_prompt(x, g, w_in, wkt, u_s, state_pad, *, tm, tk, q_scale, past):
    B, S, D = x.shape
    NB, t_new, W = u_s.shape
    d_split = w_in.shape[1] // 6
    nt = S // tm
    assert NB % (B * nt) == 0, "the sample sequences' pooling windows ride on the prompt tile steps"
    spt = NB // (B * nt)
    flat_spec = pl.BlockSpec((tm, d_split), lambda b, i: (b * nt + i, 0))
    f32_tok = jax.ShapeDtypeStruct((B * S, d_split), F32)
    bf_tok = jax.ShapeDtypeStruct((B * S, d_split), BF16)
    const = lambda a: pl.BlockSpec(a.shape, lambda b, i: (0,) * a.ndim)
    seq_spec = lambda r: pl.BlockSpec((spt, r, W), lambda b, i: (b * nt + i, 0, 0))
    body = functools.partial(_inproj_prompt_body, tm=tm, tk=tk, d_split=d_split, q_scale=q_scale, nt=nt, t_new=t_new,
                             past=past)
    return pl.pallas_call(
        body,
        grid=(B, nt),
        in_specs=[pl.BlockSpec((tm, D), lambda b, i: (b * nt + i, 0)), const(g), const(w_in), const(wkt),
                  seq_spec(t_new), seq_spec(HALO)],
        out_specs=[flat_spec,
                   pl.BlockSpec((1, HALO, d_split), lambda b, i: (b, 0, 0)),
                   flat_spec, flat_spec,
                   pl.BlockSpec((1, d_split, tm), lambda b, i: (b, 0, i)),
                   pl.BlockSpec((tm * N_HEADS, LANES), lambda b, i: (b * nt + i, 0)),
                   flat_spec,
                   pl.BlockSpec((1, tm // tk, d_split, tk), lambda b, i: (b, i, 0, 0)),
                   flat_spec, seq_spec(t_new)],
        out_shape=[bf_tok,
                   jax.ShapeDtypeStruct((B, HALO, d_split), F32),
                   f32_tok, bf_tok,
                   jax.ShapeDtypeStruct((B, d_split, S), F32),
                   jax.ShapeDtypeStruct((B * S * N_HEADS, LANES), F32),
                   f32_tok,
                   jax.ShapeDtypeStruct((B, S // tk, d_split, tk), BF16),
                   bf_tok, jax.ShapeDtypeStruct((NB, t_new, W), F32)],
        scratch_shapes=[pltpu.VMEM((HALO + tm, d_split), F32), pltpu.VMEM((HALO + t_new, W), F32)],
        compiler_params=pltpu.CompilerParams(dimension_semantics=("arbitrary", "arbitrary"),
                                             vmem_limit_bytes=VMEM_LIMIT),
        name="inproj_prompt",
    )(x.reshape(B * S, D), g, w_in, wkt, u_s, state_pad)


class _PagedAttention:
    def __init__(self, step, n_steps, pt_ref, qs_ref, kn_ref, vn_ref, kt_hbm, vr_hbm, os_ref, kbuf, vbuf, sems, qbd_ref,
                 m_ref, l_ref, acc_ref, *, n_chunks, cp, page, t_new, past, lam_init):
        self.__dict__.update(locals())
        self.ck = cp * page
        self.grp = 2 * t_new
        self.rows = N_HEADS * self.grp
        self.steps_per_seq = n_chunks // 2
        self.part = step % self.steps_per_seq
        r_i = lax.broadcasted_iota(jnp.int32, (self.rows, 1), 0)
        self.t_row = r_i % t_new
        self.slope_row = LOG2E * jnp.exp2(-8.0 * (r_i // self.grp + 1).astype(F32) / N_HEADS)

    def _copies(self, n, slot):
        out = []
        for p in range(self.cp):
            pg = self.pt_ref[n * self.cp + p]
            out.append(pltpu.make_async_copy(self.kt_hbm.at[pg], self.kbuf.at[slot, :, pl.ds(p * self.page, self.page)],
                                             self.sems.at[0, slot]))
            out.append(pltpu.make_async_copy(
                self.vr_hbm.at[pg], self.vbuf.at[slot, pl.ds(p * self.page * N_HEADS, self.page * N_HEADS), :],
                self.sems.at[1, slot]))
        return out

    def _head_dots(self, p, v_of_head):
        g = self.grp
        return jnp.concatenate(
            [jnp.dot(p[h * g:(h + 1) * g], v_of_head(h), preferred_element_type=F32) for h in range(N_HEADS)], axis=0)

    def prime(self):
        @pl.when(self.step == 0)
        def _():
            for t in range(2):
                for cpy in self._copies(t, t):
                    cpy.start()

    def begin_sequence(self):
        @pl.when(self.part == 0)
        def _():
            rows, t_new, W = self.rows, self.t_new, self.qs_ref.shape[-1]
            qf = self.qs_ref[0].astype(F32)
            qt = jnp.concatenate([qf] * (2 * N_HEADS), axis=0)
            col_grp = lax.broadcasted_iota(jnp.int32, (rows, W), 1) // (W // (2 * N_HEADS))
            row_grp = lax.broadcasted_iota(jnp.int32, (rows, W), 0) // t_new
            qbd = jnp.where(col_grp == row_grp, qt, 0.0).astype(BF16)
            self.qbd_ref[...] = qbd
            s = lax.dot_general(qbd, self.kn_ref[0].astype(BF16), (((1,), (1,)), ((), ())),
                                preferred_element_type=F32)
            kcol = lax.broadcasted_iota(jnp.int32, (rows, t_new), 1)
            s = jnp.where(kcol <= self.t_row, s + self.slope_row * kcol.astype(F32), -jnp.inf)
            m0 = jnp.max(s, axis=1, keepdims=True)
            p = jnp.exp2(s - m0)
            lane = lax.broadcasted_iota(jnp.int32, (rows, LANES), 1)
            self.m_ref[...] = jnp.broadcast_to(m0, (rows, LANES))
            self.l_ref[...] = jnp.where(lane == 0, jnp.sum(p, axis=1, keepdims=True), 0.0)
            vn = self.vn_ref[0].astype(BF16)
            self.acc_ref[...] = self._head_dots(p.astype(BF16), lambda h: vn[:, h * LANES:(h + 1) * LANES])

    def chunk(self, t):
        n = self.step * 2 + t
        for cpy in self._copies(n, t):
            cpy.wait()
        rows, ck = self.rows, self.ck
        s = jnp.dot(self.qbd_ref[...], self.kbuf[t].astype(BF16), preferred_element_type=F32)
        first = (self.part * 2 + t) * ck - self.past
        kpos = lax.broadcasted_iota(jnp.int32, (1, ck), 1).astype(F32) + first.astype(F32)
        s = s + self.slope_row * kpos
        m_old = self.m_ref[...]
        m_new = jnp.maximum(m_old, jnp.max(s, axis=1, keepdims=True))
        alpha = jnp.exp2(m_old - m_new)
        ps = [jnp.exp2(s[:, u * LANES:(u + 1) * LANES] - m_new) for u in range(ck // LANES)]
        self.l_ref[...] = alpha * self.l_ref[...] + functools.reduce(lambda x, y: x + y, ps)
        vslot = self.vbuf.at[t]
        p = jnp.concatenate(ps, axis=1).astype(BF16)
        pv = self._head_dots(p, lambda h: vslot[pl.ds(h, ck, stride=N_HEADS), :].astype(BF16))
        self.acc_ref[...] = alpha * self.acc_ref[...] + pv
        self.m_ref[...] = m_new

        @pl.when(n + 2 < self.n_steps * 2)
        def _():
            for cpy in self._copies(n + 2, t):
                cpy.start()

    def end_sequence(self, lam, hn):
        @pl.when(self.part == self.steps_per_seq - 1)
        def _():
            g, t_new = self.grp, self.t_new
            acc = self.acc_ref[...]
            l_all = jnp.sum(self.l_ref[...], axis=1, keepdims=True)
            for h in range(N_HEADS):
                r1 = slice(h * g, h * g + t_new)
                r2 = slice(h * g + t_new, (h + 1) * g)
                o = _head_finish(acc[r1], l_all[r1], acc[r2], l_all[r2], lam, hn, 1.0 - self.lam_init)
                self.os_ref[0, :, h * LANES:(h + 1) * LANES] = o


def _attn_prompt_body(pt_ref, lq1, lk1, lq2, lk2, hn_ref, q_ref, kt_ref, v_ref, qs_ref, kn_ref, vn_ref, kt_hbm, vr_hbm,
                      h_ref, pooled_ref, zp_ref, za_ref, p_ref, wpool_ref, spool_ref, wout_ref, wgate_ref, wproj_ref,
                      nf_ref,
                      y_ref, os_ref, q2_ref, m_ref, l_ref, acc_ref, kbuf, vbuf, sems, qbd_ref, ms_ref, ls_ref, accs_ref,
                      *, tq, tk, lam_init, sample_kw):
    i = pl.program_id(1)
    nq = pl.num_programs(1)
    step = pl.program_id(0) * nq + i
    paged = _PagedAttention(step, pl.num_programs(0) * nq, pt_ref, qs_ref, kn_ref, vn_ref, kt_hbm, vr_hbm, os_ref,
                            kbuf, vbuf, sems, qbd_ref, ms_ref, ls_ref, accs_ref, lam_init=lam_init, **sample_kw)
    paged.prime()
    paged.begin_sequence()
    paged.chunk(0)

    rows = 2 * tq
    half = LANES // 2
    lane = lax.broadcasted_iota(jnp.int32, (tq, LANES), 1)
    for h in range(N_HEADS):
        qh = q_ref[0, :, h * LANES:(h + 1) * LANES]
        zero = jnp.zeros_like(qh)
        q2_ref[h] = jnp.concatenate([jnp.where(lane < half, qh, zero), jnp.where(lane >= half, qh, zero)], axis=0)
    m_ref[...] = jnp.full(m_ref.shape, -jnp.inf, F32)
    l_ref[...] = jnp.zeros(l_ref.shape, F32)
    acc_ref[...] = jnp.zeros(acc_ref.shape, F32)
    col = lax.broadcasted_iota(jnp.int32, (1, tk), 1).astype(F32)
    n_full = (i * tq) // tk

    def block(j, masked):
        first = j * tk - i * tq
        kpos = col + first.astype(F32)
        for h in range(N_HEADS):
            s = jnp.dot(q2_ref[h], kt_ref[0, j, h * LANES:(h + 1) * LANES, :], preferred_element_type=F32)
            s = s + (_alibi_slope(h) * LOG2E) * kpos
            if masked:
                r_i = lax.broadcasted_iota(jnp.int32, (rows, tk), 0)
                c_i = lax.broadcasted_iota(jnp.int32, (rows, tk), 1)
                s = jnp.where(c_i + first <= jnp.where(r_i >= tq, r_i - tq, r_i), s, -jnp.inf)
            m_old = m_ref[h]
            m_new = jnp.maximum(m_old, jnp.max(s, axis=1, keepdims=True))
            alpha = jnp.exp2(m_old - m_new)
            ps = [jnp.exp2(s[:, t * LANES:(t + 1) * LANES] - m_new) for t in range(tk // LANES)]
            l_ref[h] = alpha * l_ref[h] + functools.reduce(lambda x, y: x + y, ps)
            vb = v_ref[0, pl.ds(pl.multiple_of(j * tk, tk), tk), h * LANES:(h + 1) * LANES]
            p = jnp.concatenate(ps, axis=1).astype(BF16)
            acc_ref[h] = alpha * acc_ref[h] + jnp.dot(p, vb, preferred_element_type=F32)
            m_ref[h] = m_new

    def full_block(j, carry):
        block(j, False)
        return carry

    lax.fori_loop(0, n_full, full_block, 0)
    paged.chunk(1)
    block(n_full, True)
    lam = _lam_value(lq1, lk1, lq2, lk2, lam_init)
    hn = hn_ref[...]
    heads = []
    for h in range(N_HEADS):
        l_h = jnp.sum(l_ref[h], axis=1, keepdims=True)
        heads.append(_head_finish(acc_ref[h, :tq], l_h[:tq], acc_ref[h, tq:], l_h[tq:], lam, hn, 1.0 - lam_init))
    y_ref[...] = _merge_math(jnp.concatenate(heads, axis=1), h_ref, pooled_ref, zp_ref, za_ref, p_ref, wpool_ref,
                             spool_ref, wout_ref, wgate_ref, wproj_ref, nf_ref)
    paged.end_sequence(lam, hn)


def _attn_prompt(page_table, lams, hn, qb, ktb, vb, q_s, k_s, v_s, cache_kt, cache_vr, merge_acts, merge_consts,
                 *, tq, cp, lam_init):
    B, S, W = qb.shape
    D = merge_acts[0].shape[1]
    NB, t_new, _ = q_s.shape
    tk = ktb.shape[-1]
    assert tk % tq == 0
    nq = S // tq
    n_pages = page_table.shape[1]
    page = cache_kt.shape[2]
    n_chunks = n_pages // cp
    assert n_pages % cp == 0 and n_chunks % 2 == 0
    steps_per_seq = n_chunks // 2
    assert NB * steps_per_seq == B * nq, "two cache chunks of one sample sequence ride on each prompt tile step"
    srows = 2 * N_HEADS * t_new
    body = functools.partial(_attn_prompt_body, tq=tq, tk=tk, lam_init=lam_init,
                             sample_kw=dict(n_chunks=n_chunks, cp=cp, page=page, t_new=t_new, past=n_pages * page))
    small = lambda a: pl.BlockSpec(a.shape, lambda b, i, pt: (0,) * a.ndim)
    seq_spec = pl.BlockSpec((1, t_new, W), lambda b, i, pt: ((b * nq + i) // steps_per_seq, 0, 0))
    tok = lambda a: pl.BlockSpec((tq, a.shape[1]), lambda b, i, pt: (b * nq + i, 0))
    rows = 2 * tq
    grid_spec = pltpu.PrefetchScalarGridSpec(
        num_scalar_prefetch=1,
        grid=(B, nq),
        in_specs=[small(l) for l in lams] + [small(hn), pl.BlockSpec((1, tq, W), lambda b, i, pt: (b, i, 0)),
                                              pl.BlockSpec((1,) + ktb.shape[1:], lambda b, i, pt: (b, 0, 0, 0)),
                                              pl.BlockSpec((1, S, W), lambda b, i, pt: (b, 0, 0)),
                                              seq_spec, seq_spec, seq_spec,
                                              pl.BlockSpec(memory_space=pl.ANY), pl.BlockSpec(memory_space=pl.ANY)]
        + [tok(a) for a in merge_acts] + [small(a) for a in merge_consts],
        out_specs=[pl.BlockSpec((tq, D), lambda b, i, pt: (b * nq + i, 0)), seq_spec],
        scratch_shapes=[pltpu.VMEM((N_HEADS, rows, LANES), BF16), pltpu.VMEM((N_HEADS, rows, LANES), F32),
                        pltpu.VMEM((N_HEADS, rows, LANES), F32), pltpu.VMEM((N_HEADS, rows, LANES), F32),
                        pltpu.VMEM((2, W, cp * page), F32), pltpu.VMEM((2, cp * page * N_HEADS, LANES), F32),
                        pltpu.SemaphoreType.DMA((2, 2)),
                        pltpu.VMEM((srows, W), BF16), pltpu.VMEM((srows, LANES), F32), pltpu.VMEM((srows, LANES), F32),
                        pltpu.VMEM((srows, LANES), F32)],
    )
    return pl.pallas_call(
        body,
        grid_spec=grid_spec,
        out_shape=[jax.ShapeDtypeStruct((B * S, D), F32), jax.ShapeDtypeStruct((NB, t_new, W), F32)],
        compiler_params=pltpu.CompilerParams(dimension_semantics=("arbitrary", "arbitrary"),
                                             vmem_limit_bytes=VMEM_LIMIT),
        name="attn_merge_prompt_paged_attn",
    )(page_table.reshape(-1), *lams, hn, qb, ktb, vb, q_s, k_s, v_s, cache_kt, cache_vr, *merge_acts, *merge_consts)


def _merge_math(o, h_ref, pooled_ref, zp_ref, za_ref, p_ref, wpool_ref, spool_ref, wout_ref, wgate_ref, wproj_ref,
                nf_ref):
    d_pool = pooled_ref.shape[-1]
    y_pool = jnp.concatenate(
        [jnp.dot(pooled_ref[:, g * LANES:(g + 1) * LANES].astype(BF16), wpool_ref[g], preferred_element_type=F32)
         for g in range(len(POOL_WINDOWS))], axis=-1) * spool_ref[...]
    ya = (y_pool * jax.nn.silu(zp_ref[...])).astype(BF16)
    yb = (o * jax.nn.silu(za_ref[...])).astype(BF16)
    h1 = (h_ref[...] + jnp.dot(ya, wout_ref[0:d_pool, :], preferred_element_type=F32)
          + jnp.dot(yb, wout_ref[d_pool:, :], preferred_element_type=F32))
    gate = jax.nn.sigmoid(jnp.dot(h1.astype(BF16), wgate_ref[...], preferred_element_type=F32))
    h2 = h1 + gate * jnp.dot(p_ref[...].astype(BF16), wproj_ref[...], preferred_element_type=F32)
    return _rmsnorm(h2, nf_ref[...])


def _merge_body(h_ref, pooled_ref, zp_ref, o_ref, za_ref, p_ref, wpool_ref, spool_ref, wout_ref, wgate_ref,
                wproj_ref, nf_ref, y_ref):
    y_ref[...] = _merge_math(o_ref[...], h_ref, pooled_ref, zp_ref, za_ref, p_ref, wpool_ref, spool_ref, wout_ref,
                             wgate_ref, wproj_ref, nf_ref)


def _merge(h, pooled, zp, o, za, p, w_pool, s_pool, w_out, w_gate, w_proj, norm_final, *, tm):
    T, D = h.shape
    row = lambda a: pl.BlockSpec((tm, a.shape[1]), lambda i: (i, 0))
    full = lambda a: pl.BlockSpec(a.shape, lambda i: (0,) * a.ndim)
    acts = (h, pooled, zp, o, za, p)
    consts = (w_pool, s_pool, w_out, w_gate, w_proj, norm_final)
    return pl.pallas_call(
        _merge_body,
        grid=(T // tm,),
        in_specs=[row(a) for a in acts] + [full(a) for a in consts],
        out_specs=pl.BlockSpec((tm, D), lambda i: (i, 0)),
        out_shape=jax.ShapeDtypeStruct((T, D), F32),
        compiler_params=pltpu.CompilerParams(dimension_semantics=("arbitrary",), vmem_limit_bytes=VMEM_LIMIT),
        name="merge",
    )(*acts, *consts)


def kernel(x_prompt, x_sample, cache_k, cache_v, state_pool, page_table, p_prompt, p_sample, norm_in, w_in, w_pool,
           s_pool, lambda_q1, lambda_k1, lambda_q2, lambda_k2, head_norm, w_out, w_ple_gate, w_ple_proj, norm_final):
    B, S, D = x_prompt.shape
    NB, T_NEW, _ = x_sample.shape
    depth = w_in.shape[0]
    assert depth == 1, "single-layer step"
    n_phys, page = cache_k.shape[1], cache_k.shape[2]
    hp = x_prompt.reshape(B * S, D)
    hs = x_sample.reshape(NB * T_NEW, D)
    outs = {}
    for i in range(depth):
        lam_init = _lambda_init(i)
        g_in = norm_in[i].reshape(1, D)
        w_in_b = w_in[i].astype(BF16)
        lams = tuple(a[i].reshape(1, -1) for a in (lambda_q1, lambda_k1, lambda_q2, lambda_k2))
        hn = head_norm[i].reshape(1, -1)
        consts = (w_pool[i].astype(BF16), s_pool[i].reshape(1, -1), w_out[i].astype(BF16),
                  w_ple_gate[i].astype(BF16), w_ple_proj[i].astype(BF16), norm_final.reshape(1, D))
        d_split = w_in.shape[-1] // 6
        attn_scale = (d_split // N_HEADS // 2) ** -0.5

        seq = lambda a: a.reshape(NB, T_NEW, -1)
        tm_s = min(TM_SAMPLE, NB * T_NEW)
        u_s, zp_s, qb_s, k_s, v_s, za_s = _inproj_sample(hs, g_in, w_in_b, tm=tm_s, q_scale=attn_scale * LOG2E)
        state_pad = jnp.pad(state_pool[i], ((0, 0), (HALO - POOL_STATE_LEN, 0), (0, 0)))

        wkt = w_in[i][:, 3 * d_split:4 * d_split].T.astype(BF16)
        pooled, tail, zp, qb, kt, vrow, za, ktb, vb, pooled_s = _inproj_prompt(
            hp.reshape(B, S, D), g_in, w_in_b, wkt, seq(u_s), state_pad,
            tm=TM_PROMPT, tk=TK_PROMPT, q_scale=attn_scale * LOG2E, past=page_table.shape[1] * page)
        y_prompt, o_s = _attn_prompt(page_table, lams, hn, qb.reshape(B, S, -1), ktb, vb.reshape(B, S, -1),
                                     seq(qb_s), seq(k_s), seq(v_s),
                                     jnp.transpose(cache_k[i], (0, 2, 3, 4, 1)).reshape(n_phys, -1, page),
                                     cache_v[i].reshape(n_phys, page * N_HEADS, -1),
                                     (hp, pooled, zp, za, p_prompt[i].reshape(B * S, -1)), consts,
                                     tq=TQ_PROMPT, cp=PAGES_PER_CHUNK, lam_init=lam_init)
        outs["k_p"] = jnp.transpose(kt.reshape(B, N_HEADS, 2, -1, S), (0, 4, 1, 2, 3))
        outs["v_p"] = vrow.reshape(B, S, N_HEADS, -1)
        outs["pool_p"] = tail[:, HALO - POOL_STATE_LEN:]

        y_sample = _merge(hs, pooled_s.reshape(NB * T_NEW, -1), zp_s, o_s.reshape(NB * T_NEW, -1), za_s,
                          p_sample[i].reshape(NB * T_NEW, -1), *consts, tm=tm_s)
        outs["k_s"] = k_s.reshape(NB, T_NEW, N_HEADS, 2, -1)
        outs["v_s"] = v_s.reshape(NB, T_NEW, N_HEADS, -1)
        outs["pool_s"] = jnp.concatenate([state_pool[i], seq(u_s)], axis=1)[:, -POOL_STATE_LEN:]
    return (y_prompt.reshape(B, S, D), y_sample.reshape(NB, T_NEW, D),
            outs["k_p"][None], outs["v_p"][None], outs["pool_p"][None],
            outs["k_s"][None], outs["v_s"][None], outs["pool_s"][None])
```
